```python
import math
import jax, jax.numpy as jnp
from jax import lax
import numpy as np

D_MODEL = 1024
BATCH = 32
SEQ = 2048
DEPTH = 2

CHUNK = 64
N_MIXERS = 2
N_FOX = (DEPTH + N_MIXERS - 1) // N_MIXERS
N_S5 = DEPTH // N_MIXERS
FOX_HEADS = 16
FOX_HEAD_DIM = 64
FOX_WIDTH = FOX_HEADS * FOX_HEAD_DIM
Q_BLOCK = 128
FORGET_BIAS_INIT = 2.0
S5_WIDTH = D_MODEL
S5_GROUP = 16
S5_GROUPS = S5_WIDTH // S5_GROUP
S5_STATE = 64
DT_MIN = 1e-3
DT_MAX = 1e-1
PLE_DIM = 256
RMS_EPS = 1e-6
NEG_INF = -1e30

kernel_name = "fox_s5_interleaved_sandwich_ple"


def rmsnorm(x, g):
    xf = x.astype(jnp.float32)
    var = jnp.mean(xf * xf, axis=-1, keepdims=True)
    return (xf * lax.rsqrt(var + RMS_EPS) * g.astype(jnp.float32)).astype(x.dtype)


def fox_mixer(h, w_in, b_f, w_out):
    bsz, seq, _ = h.shape
    proj = h @ w_in
    q, k, v, z, fl = jnp.split(
        proj, [FOX_WIDTH, 2 * FOX_WIDTH, 3 * FOX_WIDTH, 4 * FOX_WIDTH], axis=-1)
    q = q.reshape(bsz, seq, FOX_HEADS, FOX_HEAD_DIM)
    k = k.reshape(bsz, seq, FOX_HEADS, FOX_HEAD_DIM)
    v = v.reshape(bsz, seq, FOX_HEADS, FOX_HEAD_DIM)
    log_f = jax.nn.log_sigmoid(fl.astype(jnp.float32) + b_f.astype(jnp.float32))
    cum = jnp.cumsum(log_f, axis=1).transpose(0, 2, 1)
    scale = FOX_HEAD_DIM ** -0.5
    outs = []
    for start in range(0, seq, Q_BLOCK):
        end = start + Q_BLOCK
        s = jnp.einsum('bqhd,bkhd->bhqk', q[:, start:end], k[:, :end]).astype(jnp.float32) * scale
        bias = cum[:, :, start:end, None] - cum[:, :, None, :end]
        qpos = jnp.arange(start, end)[:, None]
        kpos = jnp.arange(end)[None, :]
        logits = jnp.where(kpos <= qpos, s + bias, NEG_INF)
        pr = jax.nn.softmax(logits, axis=-1)
        outs.append(jnp.einsum('bhqk,bkhd->bqhd', pr.astype(v.dtype), v[:, :end]))
    o = jnp.concatenate(outs, axis=1).reshape(bsz, seq, FOX_WIDTH)
    return (o * jax.nn.silu(z)) @ w_out


def _ssm_combine(e1, e2):
    a1r, a1i, b1r, b1i = e1
    a2r, a2i, b2r, b2i = e2
    ar = a2r * a1r - a2i * a1i
    ai = a2r * a1i + a2i * a1r
    br = a2r * b1r - a2i * b1i + b2r
    bi = a2r * b1i + a2i * b1r + b2i
    return (ar, ai, br, bi)


def s5_mixer(h, w_in, lam_re, lam_im, log_dt, b_re, b_im, c_re, c_im, d_skip, w_glu, b_glu, w_out):
    bsz, seq, _ = h.shape
    f32 = jnp.float32
    proj = h @ w_in
    u, z = jnp.split(proj, [S5_WIDTH], axis=-1)
    u32 = u.astype(f32).reshape(bsz, seq, S5_GROUPS, S5_GROUP)
    lr = lam_re.astype(f32)
    li = lam_im.astype(f32)
    dt = jnp.exp(log_dt.astype(f32))[:, None]
    mag = jnp.exp(lr * dt)
    lb_re = mag * jnp.cos(li * dt)
    lb_im = mag * jnp.sin(li * dt)
    den = lr * lr + li * li
    num_re = lb_re - 1.0
    num_im = lb_im
    coef_re = (num_re * lr + num_im * li) / den
    coef_im = (num_im * lr - num_re * li) / den
    br = b_re.astype(f32)
    bi = b_im.astype(f32)
    bb_re = coef_re[..., None] * br - coef_im[..., None] * bi
    bb_im = coef_re[..., None] * bi + coef_im[..., None] * br
    bu_re = jnp.einsum('blgi,gpi->lbgp', u32, bb_re)
    bu_im = jnp.einsum('blgi,gpi->lbgp', u32, bb_im)
    a_re = jnp.broadcast_to(lb_re[None, None], (seq, 1, S5_GROUPS, S5_STATE))
    a_im = jnp.broadcast_to(lb_im[None, None], (seq, 1, S5_GROUPS, S5_STATE))
    _, _, h_re, h_im = lax.associative_scan(_ssm_combine, (a_re, a_im, bu_re, bu_im), axis=0)
    y = (jnp.einsum('lbgp,gip->blgi', h_re, c_re.astype(f32))
         - jnp.einsum('lbgp,gip->blgi', h_im, c_im.astype(f32)))
    y = y.reshape(bsz, seq, S5_WIDTH) + d_skip.astype(f32) * u.astype(f32)
    y = y.astype(h.dtype)
    g = jax.nn.gelu(y)
    y = g * jax.nn.sigmoid(g @ w_glu + b_glu)
    return (y * jax.nn.silu(z)) @ w_out


def setup_inputs(seed: int = 0) -> dict:
    key = jax.random.key(seed)
    ks = jax.random.split(key, 24)
    nrm = jax.random.normal
    d = D_MODEL
    x = nrm(ks[0], (BATCH, SEQ, d), jnp.float32)
    p = nrm(ks[1], (DEPTH, BATCH, SEQ, PLE_DIM), jnp.float32)
    ln_pre = 1.0 + 0.05 * nrm(ks[2], (DEPTH, d), jnp.float32)
    ln_post = 1.0 + 0.05 * nrm(ks[3], (DEPTH, d), jnp.float32)
    fox_w_in = nrm(ks[4], (N_FOX, d, 4 * FOX_WIDTH + FOX_HEADS), jnp.float32) * d ** -0.5
    fox_b_f = FORGET_BIAS_INIT + 0.5 * nrm(ks[5], (N_FOX, FOX_HEADS), jnp.float32)
    fox_w_out = nrm(ks[6], (N_FOX, FOX_WIDTH, d), jnp.float32) * FOX_WIDTH ** -0.5
    s5_w_in = nrm(ks[7], (N_S5, d, 2 * S5_WIDTH), jnp.float32) * d ** -0.5
    s5_lambda_re = -0.5 * jnp.exp(0.02 * nrm(ks[8], (N_S5, S5_GROUPS, S5_STATE), jnp.float32))
    s5_lambda_im = (math.pi * jnp.arange(S5_STATE, dtype=jnp.float32)[None, None, :]
                    + 0.01 * nrm(ks[9], (N_S5, S5_GROUPS, S5_STATE), jnp.float32))
    s5_log_dt = jax.random.uniform(ks[10], (N_S5, S5_GROUPS), jnp.float32,
                                   minval=math.log(DT_MIN), maxval=math.log(DT_MAX))
    bscale = (2.0 * S5_GROUP) ** -0.5
    s5_b_re = nrm(ks[11], (N_S5, S5_GROUPS, S5_STATE, S5_GROUP), jnp.float32) * bscale
    s5_b_im = nrm(ks[12], (N_S5, S5_GROUPS, S5_STATE, S5_GROUP), jnp.float32) * bscale
    cscale = S5_STATE ** -0.5
    s5_c_re = nrm(ks[13], (N_S5, S5_GROUPS, S5_GROUP, S5_STATE), jnp.float32) * cscale
    s5_c_im = nrm(ks[14], (N_S5, S5_GROUPS, S5_GROUP, S5_STATE), jnp.float32) * cscale
    s5_d = nrm(ks[15], (N_S5, S5_WIDTH), jnp.float32)
    s5_w_glu = nrm(ks[16], (N_S5, S5_WIDTH, S5_WIDTH), jnp.float32) * S5_WIDTH ** -0.5
    s5_b_glu = 0.01 * nrm(ks[17], (N_S5, S5_WIDTH), jnp.float32)
    s5_w_out = nrm(ks[18], (N_S5, S5_WIDTH, d), jnp.float32) * S5_WIDTH ** -0.5
    ple_w_proj = nrm(ks[19], (DEPTH, PLE_DIM, d), jnp.float32) * PLE_DIM ** -0.5
    ple_w_gate = nrm(ks[20], (DEPTH, d, d), jnp.float32) * d ** -0.5
    ple_b_gate = 0.01 * nrm(ks[21], (DEPTH, d), jnp.float32)
    return {"x": x, "p": p, "ln_pre": ln_pre, "ln_post": ln_post,
            "fox_w_in": fox_w_in, "fox_b_f": fox_b_f, "fox_w_out": fox_w_out,
            "s5_w_in": s5_w_in, "s5_lambda_re": s5_lambda_re, "s5_lambda_im": s5_lambda_im,
            "s5_log_dt": s5_log_dt, "s5_b_re": s5_b_re, "s5_b_im": s5_b_im,
            "s5_c_re": s5_c_re, "s5_c_im": s5_c_im, "s5_d": s5_d,
            "s5_w_glu": s5_w_glu, "s5_b_glu": s5_b_glu, "s5_w_out": s5_w_out,
            "ple_w_proj": ple_w_proj, "ple_w_gate": ple_w_gate, "ple_b_gate": ple_b_gate}


def reference(x, p, ln_pre, ln_post, fox_w_in, fox_b_f, fox_w_out, s5_w_in, s5_lambda_re,
              s5_lambda_im, s5_log_dt, s5_b_re, s5_b_im, s5_c_re, s5_c_im, s5_d, s5_w_glu,
              s5_b_glu, s5_w_out, ple_w_proj, ple_w_gate, ple_b_gate):
    for i in range(DEPTH):
        h = rmsnorm(x, ln_pre[i])
        j = i // N_MIXERS
        if i % N_MIXERS == 0:
            y = fox_mixer(h, fox_w_in[j], fox_b_f[j], fox_w_out[j])
        else:
            y = s5_mixer(h, s5_w_in[j], s5_lambda_re[j], s5_lambda_im[j], s5_log_dt[j],
                         s5_b_re[j], s5_b_im[j], s5_c_re[j], s5_c_im[j], s5_d[j],
                         s5_w_glu[j], s5_b_glu[j], s5_w_out[j])
        x = x + rmsnorm(y, ln_post[i])
        gate = jax.nn.sigmoid(x @ ple_w_gate[i] + ple_b_gate[i])
        x = x + (p[i] @ ple_w_proj[i]) * gate
    return x
```

```python
import functools
import math

import jax
import jax.numpy as jnp
import numpy as np
from jax import lax
from jax.experimental import pallas as pl
from jax.experimental.pallas import tpu as pltpu

F32 = jnp.float32
BF16 = jnp.bfloat16

D_MODEL = 1024
PLE_DIM = 256
RMS_EPS = 1e-6
NEG_INF = -1e30

FOX_HEADS = 16
FOX_HEAD_DIM = 64
FOX_WIDTH = FOX_HEADS * FOX_HEAD_DIM
FOX_AUG_DIM = 128
FOX_AUG_WIDTH = FOX_HEADS * FOX_AUG_DIM

S5_GROUP = 16
S5_GROUPS = D_MODEL // S5_GROUP
S5_STATE = 64
S5_CHUNK = 16
S5_ROW = S5_CHUNK * S5_GROUP

V7X_SCOPED_VMEM_LIMIT_BYTES = 56 * 1024 * 1024

PROJ_ROWS = 256
POST_ROWS = 512
ATTN_Q_ROWS = 256
ATTN_KV_ROWS = 256


def _const_spec(shape):
    zeros = (0,) * len(shape)
    return pl.BlockSpec(shape, lambda *_: zeros, pipeline_mode=pl.Buffered(1))


def _params(*semantics):
    return pltpu.CompilerParams(dimension_semantics=semantics,
                                vmem_limit_bytes=V7X_SCOPED_VMEM_LIMIT_BYTES)


def _dot(a, b):
    return jnp.dot(a, b, preferred_element_type=F32)


def _rmsnorm(x, g):
    var = jnp.mean(x * x, axis=-1, keepdims=True)
    return x * lax.rsqrt(var + RMS_EPS) * g


def _sigmoid(x):
    return 1.0 / (1.0 + jnp.exp(-x))


def _silu(x):
    return x * _sigmoid(x)


def _gelu_tanh(x):
    c = math.sqrt(2.0 / math.pi)
    return 0.5 * x * (1.0 + jnp.tanh(c * (x + 0.044715 * (x * x * x))))


def _split3(x):
    hi = x.astype(BF16).astype(F32)
    r1 = x - hi
    mid = r1.astype(BF16).astype(F32)
    lo = (r1 - mid).astype(BF16).astype(F32)
    return hi, mid, lo


def _fox_proj_kernel(x_ref, g_ref, wq_ref, wk_ref, wv_ref, wz_ref, wf_ref, bf_ref, eq_ref, ek_ref,
                     q_out, k_out, v_out, z_out, carry_ref):
    rows = x_ref.shape[1]

    @pl.when(pl.program_id(1) == 0)
    def _():
        carry_ref[...] = jnp.zeros_like(carry_ref)

    h = _rmsnorm(x_ref[0], g_ref[...]).astype(BF16)

    zf = _dot(h, wf_ref[...]) + bf_ref[...]
    log_f = jnp.minimum(zf, 0.0) - jnp.log(1.0 + jnp.exp(-jnp.abs(zf)))
    r = lax.broadcasted_iota(jnp.int32, (rows, rows), 0)
    c = lax.broadcasted_iota(jnp.int32, (rows, rows), 1)
    tri = jnp.where(r >= c, 1.0, 0.0).astype(BF16)
    hi, mid, lo = _split3(log_f)
    cum = (_dot(tri, hi.astype(BF16)) + _dot(tri, mid.astype(BF16))
           + _dot(tri, lo.astype(BF16))) + carry_ref[0:1, :]
    carry_ref[...] = jnp.broadcast_to(cum[rows - 1:rows, :], carry_ref.shape)

    c_hi, c_mid, c_lo = _split3(cum)
    lane = lax.broadcasted_iota(jnp.int32, cum.shape, 1)
    aug = jnp.where(lane < 16, c_hi,
                    jnp.where(lane < 32, c_mid,
                              jnp.where(lane < 48, c_lo,
                                        jnp.where(lane == 48, 1.0, 0.0)))).astype(BF16)

    q_out[0] = (_dot(h, wq_ref[...]) + _dot(aug, eq_ref[...])).astype(BF16)
    k_out[0] = (_dot(h, wk_ref[...]) + _dot(aug, ek_ref[...])).astype(BF16)
    v_out[0] = _dot(h, wv_ref[...]).astype(BF16)
    z_out[0] = _dot(h, wz_ref[...]).astype(BF16)


def _fox_aug_placement():
    eq = np.zeros((128, FOX_AUG_WIDTH), np.float32)
    ek = np.zeros((128, FOX_AUG_WIDTH), np.float32)
    for h in range(FOX_HEADS):
        base = h * FOX_AUG_DIM + FOX_HEAD_DIM
        for part in range(3):
            eq[part * 16 + h, base + part] = 1.0
            ek[part * 16 + h, base + 3 + part] = -1.0
            eq[48, base + 3 + part] = 1.0
            ek[48, base + part] = 1.0
    return jnp.asarray(eq, BF16), jnp.asarray(ek, BF16)


def _fox_proj(x, g, w_in, b_f):
    bsz, seq, d = x.shape
    scale = FOX_HEAD_DIM ** -0.5
    pad = ((0, 0), (0, 0), (0, FOX_AUG_DIM - FOX_HEAD_DIM))

    def widen(w, s):
        w = (w * s).reshape(d, FOX_HEADS, FOX_HEAD_DIM)
        return jnp.pad(w, pad).reshape(d, FOX_AUG_WIDTH).astype(BF16)

    wq = widen(w_in[:, :FOX_WIDTH], scale)
    wk = widen(w_in[:, FOX_WIDTH:2 * FOX_WIDTH], 1.0)
    wv = w_in[:, 2 * FOX_WIDTH:3 * FOX_WIDTH].astype(BF16)
    wz = w_in[:, 3 * FOX_WIDTH:4 * FOX_WIDTH].astype(BF16)
    wf = w_in[:, 4 * FOX_WIDTH:]
    wf = jnp.pad(jnp.tile(wf, (1, 3)), ((0, 0), (0, 128 - 3 * FOX_HEADS))).astype(BF16)
    bf = jnp.pad(jnp.tile(b_f, 3), (0, 128 - 3 * FOX_HEADS)).reshape(1, 128).astype(F32)
    eq, ek = _fox_aug_placement()

    t = PROJ_ROWS
    row_spec = lambda w: pl.BlockSpec((1, t, w), lambda b, i: (b, i, 0))
    return pl.pallas_call(
        _fox_proj_kernel,
        grid=(bsz, seq // t),
        in_specs=[row_spec(d), _const_spec((1, d)),
                  _const_spec(wq.shape), _const_spec(wk.shape), _const_spec(wv.shape),
                  _const_spec(wz.shape), _const_spec(wf.shape), _const_spec(bf.shape),
                  _const_spec(eq.shape), _const_spec(ek.shape)],
        out_specs=[row_spec(FOX_AUG_WIDTH), row_spec(FOX_AUG_WIDTH), row_spec(FOX_WIDTH),
                   row_spec(FOX_WIDTH)],
        out_shape=[jax.ShapeDtypeStruct((bsz, seq, FOX_AUG_WIDTH), BF16),
                   jax.ShapeDtypeStruct((bsz, seq, FOX_AUG_WIDTH), BF16),
                   jax.ShapeDtypeStruct((bsz, seq, FOX_WIDTH), BF16),
                   jax.ShapeDtypeStruct((bsz, seq, FOX_WIDTH), BF16)],
        scratch_shapes=[pltpu.VMEM((8, 128), F32)],
        compiler_params=_params("parallel", "arbitrary"),
        name="fox_proj",
    )(x, g.reshape(1, d), wq, wk, wv, wz, wf, bf, eq, ek)


def _fox_attn_kernel(q_ref, k_ref, v_ref, z_ref, o_ref):
    tq = q_ref.shape[1]
    tk = ATTN_KV_ROWS
    i = pl.program_id(2)
    heads = (0, 1)
    qs = [q_ref[0, :, a * FOX_AUG_DIM:(a + 1) * FOX_AUG_DIM] for a in heads]

    def update(a, start, carry, masked):
        m, l, acc = carry
        k = k_ref[0, pl.ds(start, tk), a * FOX_AUG_DIM:(a + 1) * FOX_AUG_DIM]
        s = lax.dot_general(qs[a], k, (((1,), (1,)), ((), ())), preferred_element_type=F32)
        if masked:
            row = lax.broadcasted_iota(jnp.int32, s.shape, 0)
            col = lax.broadcasted_iota(jnp.int32, s.shape, 1)
            s = jnp.where(col <= row, s, NEG_INF)
        m_new = jnp.maximum(m, jnp.max(s, axis=1, keepdims=True))
        alpha = jnp.exp(m - m_new)
        p = jnp.exp(s - m_new)
        l = alpha * l + jnp.sum(p, axis=1, keepdims=True)
        acc = alpha * acc + _dot(p.astype(BF16), v_ref[0, pl.ds(start, tk), :])
        return m_new, l, acc

    def body(kb, carry):
        start = pl.multiple_of(kb * tk, tk)
        return tuple(update(a, start, carry[a], masked=False) for a in heads)

    init = tuple((jnp.full((tq, 1), NEG_INF, F32), jnp.zeros((tq, 1), F32),
                  jnp.zeros((tq, 2 * FOX_HEAD_DIM), F32)) for _ in heads)
    carry = lax.fori_loop(0, i, body, init)
    diag = pl.multiple_of(i * tk, tk)
    outs = []
    for a in heads:
        _, l, acc = update(a, diag, carry[a], masked=True)
        outs.append(acc / l)
    lane = lax.broadcasted_iota(jnp.int32, outs[0].shape, 1)
    o = jnp.where(lane < FOX_HEAD_DIM, outs[0], outs[1])
    o_ref[0] = (o * _silu(z_ref[0].astype(F32))).astype(BF16)


def _fox_attn(qa, ka, v, z):
    bsz, seq, _ = v.shape
    tq = ATTN_Q_ROWS
    assert ATTN_Q_ROWS == ATTN_KV_ROWS
    pair = 2 * FOX_HEAD_DIM
    return pl.pallas_call(
        _fox_attn_kernel,
        grid=(bsz, FOX_HEADS // 2, seq // tq),
        in_specs=[pl.BlockSpec((1, tq, 2 * FOX_AUG_DIM), lambda b, j, i: (b, i, j)),
                  pl.BlockSpec((1, seq, 2 * FOX_AUG_DIM), lambda b, j, i: (b, 0, j)),
                  pl.BlockSpec((1, seq, pair), lambda b, j, i: (b, 0, j)),
                  pl.BlockSpec((1, tq, pair), lambda b, j, i: (b, i, j))],
        out_specs=pl.BlockSpec((1, tq, pair), lambda b, j, i: (b, i, j)),
        out_shape=jax.ShapeDtypeStruct((bsz, seq, FOX_WIDTH), BF16),
        compiler_params=_params("parallel", "parallel", "arbitrary"),
        name="fox_attn",
    )(qa, ka, v, z)


def _s5_proj_kernel(x_ref, g_ref, wu_ref, wz_ref, u_out, z_out):
    h = _rmsnorm(x_ref[0], g_ref[...]).astype(BF16)
    u_out[0] = _dot(h, wu_ref[...]).astype(BF16)
    z_out[0] = _dot(h, wz_ref[...]).astype(BF16)


def _s5_proj(x, g, w_in):
    bsz, seq, d = x.shape
    wu = w_in[:, :d].astype(BF16)
    wz = w_in[:, d:].astype(BF16)
    t = PROJ_ROWS
    row_spec = pl.BlockSpec((1, t, d), lambda b, i: (b, i, 0))
    return pl.pallas_call(
        _s5_proj_kernel,
        grid=(bsz, seq // t),
        in_specs=[row_spec, _const_spec((1, d)), _const_spec(wu.shape), _const_spec(wz.shape)],
        out_specs=[row_spec, row_spec],
        out_shape=[jax.ShapeDtypeStruct((bsz, seq, d), BF16)] * 2,
        compiler_params=_params("parallel", "parallel"),
        name="s5_proj",
    )(x, g.reshape(1, d), wu, wz)


def _s5_group_kernel(x_ref, w_ref, bre_ref, bim_ref, cre_ref, cim_ref, ar_ref, ai_ref, d_ref,
                     y_ref, sre_ref, sim_ref, pre_ref, pim_ref, *, batch, chunks):
    x = x_ref[0]
    sre_ref[...] = _dot(x, bre_ref[0])
    sim_ref[...] = _dot(x, bim_ref[0])
    ar = ar_ref[0]
    ai = ai_ref[0]

    def body(c, carry):
        xr, xi = carry
        rows = pl.ds(pl.multiple_of(c * batch, batch), batch)
        pre_ref[rows, :] = xr
        pim_ref[rows, :] = xi
        nxr = ar * xr - ai * xi + sre_ref[rows, :]
        nxi = ar * xi + ai * xr + sim_ref[rows, :]
        return nxr, nxi

    zero = jnp.zeros((batch, S5_STATE), F32)
    lax.fori_loop(0, chunks, body, (zero, zero))

    y = _dot(x, w_ref[0])
    y += _dot(pre_ref[...].astype(BF16), cre_ref[0])
    y += _dot(pim_ref[...].astype(BF16), cim_ref[0])
    y_ref[0] = y + d_ref[0] * x.astype(F32)


def _s5_operators(lam_re, lam_im, log_dt, b_re, b_im, c_re, c_im, d_skip):
    hp = lax.Precision.HIGHEST
    t = S5_CHUNK
    dt = jnp.exp(log_dt)[:, None]
    tau = jnp.arange(t + 1, dtype=F32)[:, None, None]
    mag = jnp.exp(lam_re * dt * tau)
    ang = lam_im * dt * tau
    pw_re = mag * jnp.cos(ang)
    pw_im = mag * jnp.sin(ang)
    lb_re, lb_im = pw_re[1], pw_im[1]
    den = lam_re * lam_re + lam_im * lam_im
    num_re = lb_re - 1.0
    num_im = lb_im
    coef_re = (num_re * lam_re + num_im * lam_im) / den
    coef_im = (num_im * lam_re - num_re * lam_im) / den
    bb_re = coef_re[..., None] * b_re - coef_im[..., None] * b_im
    bb_im = coef_re[..., None] * b_im + coef_im[..., None] * b_re
    m_re = pw_re[..., None] * bb_re - pw_im[..., None] * bb_im
    m_im = pw_re[..., None] * bb_im + pw_im[..., None] * bb_re
    kern = (jnp.einsum('gop,tgpi->tgoi', c_re, m_re[:t], precision=hp)
            - jnp.einsum('gop,tgpi->tgoi', c_im, m_im[:t], precision=hp))
    lag = jnp.arange(t)[None, :] - jnp.arange(t)[:, None]
    toep = jnp.where((lag >= 0)[:, :, None, None, None], kern[jnp.maximum(lag, 0)], 0.0)
    w = toep.transpose(2, 0, 4, 1, 3).reshape(S5_GROUPS, S5_ROW, S5_ROW)
    bst_re = m_re[:t][::-1].transpose(1, 0, 3, 2).reshape(S5_GROUPS, S5_ROW, S5_STATE)
    bst_im = m_im[:t][::-1].transpose(1, 0, 3, 2).reshape(S5_GROUPS, S5_ROW, S5_STATE)
    pr = pw_re[1:].transpose(1, 2, 0)[:, :, :, None]
    pi = pw_im[1:].transpose(1, 2, 0)[:, :, :, None]
    cr = c_re.transpose(0, 2, 1)[:, :, None, :]
    ci = c_im.transpose(0, 2, 1)[:, :, None, :]
    cst_re = (cr * pr - ci * pi).reshape(S5_GROUPS, S5_STATE, S5_ROW)
    cst_im = (-(cr * pi + ci * pr)).reshape(S5_GROUPS, S5_STATE, S5_ROW)
    a_re = pw_re[t].reshape(S5_GROUPS, 1, S5_STATE)
    a_im = pw_im[t].reshape(S5_GROUPS, 1, S5_STATE)
    d_row = jnp.tile(d_skip.reshape(S5_GROUPS, 1, S5_GROUP), (1, 1, t))
    return (w.astype(BF16), bst_re.astype(BF16), bst_im.astype(BF16), cst_re.astype(BF16),
            cst_im.astype(BF16), a_re, a_im, d_row)


def _s5_ssm(u, ops):
    bsz, seq, _ = u.shape
    chunks = seq // S5_CHUNK
    rows = chunks * bsz
    xg = u.reshape(bsz, chunks, S5_CHUNK, S5_GROUPS, S5_GROUP).transpose(3, 1, 0, 2, 4)
    xg = xg.reshape(S5_GROUPS, rows, S5_ROW)
    g_spec = lambda a, b: pl.BlockSpec((1, a, b), lambda g: (g, 0, 0))
    yg = pl.pallas_call(
        functools.partial(_s5_group_kernel, batch=bsz, chunks=chunks),
        grid=(S5_GROUPS,),
        in_specs=[g_spec(rows, S5_ROW), g_spec(S5_ROW, S5_ROW),
                  g_spec(S5_ROW, S5_STATE), g_spec(S5_ROW, S5_STATE),
                  g_spec(S5_STATE, S5_ROW), g_spec(S5_STATE, S5_ROW),
                  g_spec(1, S5_STATE), g_spec(1, S5_STATE), g_spec(1, S5_ROW)],
        out_specs=g_spec(rows, S5_ROW),
        out_shape=jax.ShapeDtypeStruct((S5_GROUPS, rows, S5_ROW), F32),
        scratch_shapes=[pltpu.VMEM((rows, S5_STATE), F32)] * 4,
        compiler_params=_params("parallel"),
        name="s5_ssm",
    )(xg, *ops)
    y = yg.reshape(S5_GROUPS, chunks, bsz, S5_CHUNK, S5_GROUP).transpose(2, 1, 3, 0, 4)
    return y.reshape(bsz, seq, S5_GROUPS * S5_GROUP)


def _post_kernel(*refs, glu):
    if glu:
        (y_ref, z_ref, wglu_ref, bglu_ref, x_ref, p_ref, wo_ref, g_ref, wg_ref, bg_ref, wp_ref,
         o_ref) = refs
        gl = _gelu_tanh(y_ref[0])
        gl = gl * _sigmoid(_dot(gl.astype(BF16), wglu_ref[...]) + bglu_ref[...])
        mix = (gl * _silu(z_ref[0].astype(F32))).astype(BF16)
    else:
        a_ref, x_ref, p_ref, wo_ref, g_ref, wg_ref, bg_ref, wp_ref, o_ref = refs
        mix = a_ref[0]
    y = _dot(mix, wo_ref[...])
    x1 = x_ref[0] + _rmsnorm(y, g_ref[...])
    gate = _sigmoid(_dot(x1.astype(BF16), wg_ref[...]) + bg_ref[...])
    o_ref[0] = x1 + _dot(p_ref[0, 0].astype(BF16), wp_ref[...]) * gate


def _post(mix_inputs, x, p, layer, w_out, g_post, w_gate, b_gate, w_proj, glu_params=None):
    bsz, seq, d = x.shape
    t = POST_ROWS
    row_spec = pl.BlockSpec((1, t, d), lambda b, i: (b, i, 0))
    p_spec = pl.BlockSpec((1, 1, t, PLE_DIM), lambda b, i: (layer, b, i, 0))
    glu = glu_params is not None
    args = list(mix_inputs)
    specs = [row_spec] * len(args)
    if glu:
        w_glu, b_glu = glu_params
        args += [w_glu.astype(BF16), b_glu.reshape(1, d)]
        specs += [_const_spec((d, d)), _const_spec((1, d))]
    args += [x, p, w_out.astype(BF16), g_post.reshape(1, d), w_gate.astype(BF16),
             b_gate.reshape(1, d), w_proj.astype(BF16)]
    specs += [row_spec, p_spec, _const_spec((d, d)), _const_spec((1, d)), _const_spec((d, d)),
              _const_spec((1, d)), _const_spec((PLE_DIM, d))]
    return pl.pallas_call(
        functools.partial(_post_kernel, glu=glu),
        grid=(bsz, seq // t),
        in_specs=specs,
        out_specs=row_spec,
        out_shape=jax.ShapeDtypeStruct((bsz, seq, d), F32),
        compiler_params=_params("parallel", "parallel"),
        name="s5_post" if glu else "fox_post",
    )(*args)


def kernel(x, p, ln_pre, ln_post, fox_w_in, fox_b_f, fox_w_out, s5_w_in, s5_lambda_re, s5_lambda_im,
           s5_log_dt, s5_b_re, s5_b_im, s5_c_re, s5_c_im, s5_d, s5_w_glu, s5_b_glu, s5_w_out,
           ple_w_proj, ple_w_gate, ple_b_gate):
    qa, ka, v, z = _fox_proj(x, ln_pre[0], fox_w_in[0], fox_b_f[0])
    gated = _fox_attn(qa, ka, v, z)
    x = _post([gated], x, p, 0, fox_w_out[0], ln_post[0], ple_w_gate[0], ple_b_gate[0],
              ple_w_proj[0])
    u, z = _s5_proj(x, ln_pre[1], s5_w_in[0])
    ops = _s5_operators(s5_lambda_re[0], s5_lambda_im[0], s5_log_dt[0], s5_b_re[0], s5_b_im[0],
                        s5_c_re[0], s5_c_im[0], s5_d[0])
    y = _s5_ssm(u, ops)
    x = _post([y, z], x, p, 1, s5_w_out[0], ln_post[1], ple_w_gate[1], ple_b_gate[1],
              ple_w_proj[1], glu_params=(s5_w_glu[0], s5_b_glu[0]))
    return x
```

```python
import functools
import math

import jax
import jax.numpy as jnp
import numpy as np
from jax import lax
from jax.experimental import pallas as pl
from jax.experimental.pallas import tpu as pltpu

F32 = jnp.float32
BF16 = jnp.bfloat16

D_MODEL = 1024
PLE_DIM = 256
RMS_EPS = 1e-6
NEG_INF = -1e30

FOX_HEADS = 16
FOX_HEAD_DIM = 64
FOX_WIDTH = FOX_HEADS * FOX_HEAD_DIM
FOX_AUG_DIM = 128
FOX_AUG_WIDTH = FOX_HEADS * FOX_AUG_DIM

S5_GROUP = 16
S5_GROUPS = D_MODEL // S5_GROUP
S5_STATE = 64
S5_CHUNK = 16
S5_ROW = S5_CHUNK * S5_GROUP

V7X_SCOPED_VMEM_LIMIT_BYTES = 56 * 1024 * 1024

PROJ_ROWS = 256
POST_ROWS = 512
ATTN_Q_ROWS = 256
ATTN_KV_ROWS = PROJ_ROWS
ATTN_SCORE_SLOTS = 3
LOG2_E = 1.4426950408889634


def _const_spec(shape):
    zeros = (0,) * len(shape)
    return pl.BlockSpec(shape, lambda *_: zeros, pipeline_mode=pl.Buffered(1))


def _params(*semantics):
    return pltpu.CompilerParams(dimension_semantics=semantics,
                                vmem_limit_bytes=V7X_SCOPED_VMEM_LIMIT_BYTES)


def _dot(a, b):
    return jnp.dot(a, b, preferred_element_type=F32)


def _rmsnorm(x, g):
    var = jnp.mean(x * x, axis=-1, keepdims=True)
    return x * lax.rsqrt(var + RMS_EPS) * g


def _sigmoid(x):
    return 1.0 / (1.0 + jnp.exp(-x))


def _silu(x):
    return x * _sigmoid(x)


def _gelu_tanh(x):
    c = math.sqrt(2.0 / math.pi)
    return 0.5 * x * (1.0 + jnp.tanh(c * (x + 0.044715 * (x * x * x))))


def _split3(x):
    hi = x.astype(BF16).astype(F32)
    r1 = x - hi
    mid = r1.astype(BF16).astype(F32)
    lo = (r1 - mid).astype(BF16).astype(F32)
    return hi, mid, lo


def _fox_proj_kernel(x_ref, g_ref, wq_ref, wk_ref, wv_ref, wz_ref, wf_ref, bf_ref, eq_ref, ek_ref,
                     q_out, k_out, v_out, z_out, carry_ref):
    rows = x_ref.shape[1]

    @pl.when(pl.program_id(1) == 0)
    def _():
        carry_ref[...] = jnp.zeros_like(carry_ref)

    h = _rmsnorm(x_ref[0], g_ref[...]).astype(BF16)

    zf = _dot(h, wf_ref[...]) + bf_ref[...]
    log_f = jnp.minimum(zf, 0.0) - jnp.log(1.0 + jnp.exp(-jnp.abs(zf)))
    r = lax.broadcasted_iota(jnp.int32, (rows, rows), 0)
    c = lax.broadcasted_iota(jnp.int32, (rows, rows), 1)
    tri = jnp.where(r >= c, 1.0, 0.0).astype(BF16)
    hi, mid, lo = _split3(log_f)
    cum = (_dot(tri, hi.astype(BF16)) + _dot(tri, mid.astype(BF16))
           + _dot(tri, lo.astype(BF16))) + carry_ref[0:1, :]
    carry_ref[...] = jnp.broadcast_to(cum[rows - 1:rows, :], carry_ref.shape)

    c_hi, c_mid, c_lo = _split3(cum * LOG2_E)
    lane = lax.broadcasted_iota(jnp.int32, cum.shape, 1)
    aug = jnp.where(lane < 16, c_hi,
                    jnp.where(lane < 32, c_mid,
                              jnp.where(lane < 48, c_lo,
                                        jnp.where(lane == 48, 1.0, 0.0)))).astype(BF16)

    q_out[0] = (_dot(h, wq_ref[...]) * LOG2_E + _dot(aug, eq_ref[...])).astype(BF16)
    k_out[0] = (_dot(h, wk_ref[...]) + _dot(aug, ek_ref[...])).astype(BF16)
    v_out[0, 0] = lax.dot_general(wv_ref[...], h, (((1,), (1,)), ((), ())),
                                  preferred_element_type=F32).astype(BF16)
    z_out[0] = _dot(h, wz_ref[...]).astype(BF16)


def _fox_aug_placement():
    eq = np.zeros((128, FOX_AUG_WIDTH), np.float32)
    ek = np.zeros((128, FOX_AUG_WIDTH), np.float32)
    for h in range(FOX_HEADS):
        base = h * FOX_AUG_DIM + FOX_HEAD_DIM
        for part in range(3):
            eq[part * 16 + h, base + part] = 1.0
            ek[part * 16 + h, base + 3 + part] = -1.0
            eq[48, base + 3 + part] = 1.0
            ek[48, base + part] = 1.0
    return jnp.asarray(eq, BF16), jnp.asarray(ek, BF16)


def _fox_proj(x, g, w_in, b_f):
    bsz, seq, d = x.shape
    scale = FOX_HEAD_DIM ** -0.5
    pad = ((0, 0), (0, 0), (0, FOX_AUG_DIM - FOX_HEAD_DIM))

    def widen(w, s):
        w = (w * s).reshape(d, FOX_HEADS, FOX_HEAD_DIM)
        return jnp.pad(w, pad).reshape(d, FOX_AUG_WIDTH).astype(BF16)

    wq = widen(w_in[:, :FOX_WIDTH], scale)
    wk = widen(w_in[:, FOX_WIDTH:2 * FOX_WIDTH], 1.0)
    wv = w_in[:, 2 * FOX_WIDTH:3 * FOX_WIDTH].T.astype(BF16)
    wz = w_in[:, 3 * FOX_WIDTH:4 * FOX_WIDTH].astype(BF16)
    wf = w_in[:, 4 * FOX_WIDTH:]
    wf = jnp.pad(jnp.tile(wf, (1, 3)), ((0, 0), (0, 128 - 3 * FOX_HEADS))).astype(BF16)
    bf = jnp.pad(jnp.tile(b_f, 3), (0, 128 - 3 * FOX_HEADS)).reshape(1, 128).astype(F32)
    eq, ek = _fox_aug_placement()

    t = PROJ_ROWS
    row_spec = lambda w: pl.BlockSpec((1, t, w), lambda b, i: (b, i, 0))
    return pl.pallas_call(
        _fox_proj_kernel,
        grid=(bsz, seq // t),
        in_specs=[row_spec(d), _const_spec((1, d)),
                  _const_spec(wq.shape), _const_spec(wk.shape), _const_spec(wv.shape),
                  _const_spec(wz.shape), _const_spec(wf.shape), _const_spec(bf.shape),
                  _const_spec(eq.shape), _const_spec(ek.shape)],
        out_specs=[row_spec(FOX_AUG_WIDTH), row_spec(FOX_AUG_WIDTH),
                   pl.BlockSpec((1, 1, FOX_WIDTH, t), lambda b, i: (b, i, 0, 0)),
                   row_spec(FOX_WIDTH)],
        out_shape=[jax.ShapeDtypeStruct((bsz, seq, FOX_AUG_WIDTH), BF16),
                   jax.ShapeDtypeStruct((bsz, seq, FOX_AUG_WIDTH), BF16),
                   jax.ShapeDtypeStruct((bsz, seq // t, FOX_WIDTH, t), BF16),
                   jax.ShapeDtypeStruct((bsz, seq, FOX_WIDTH), BF16)],
        scratch_shapes=[pltpu.VMEM((8, 128), F32)],
        compiler_params=_params("parallel", "arbitrary"),
        name="fox_proj",
    )(x, g.reshape(1, d), wq, wk, wv, wz, wf, bf, eq, ek)


def _fox_attn_kernel(q_ref, k_ref, vt_ref, z_ref, o_ref, st_ref, *, tq, tk):
    seq = q_ref.shape[1]
    heads = (0, 1)
    slots = st_ref.shape[0]
    steps = [(i, kb) for i in range(seq // tq) for kb in range(i + 1)]

    def issue_scores(t):
        i, kb = steps[t]
        for a in heads:
            lanes = slice(a * FOX_AUG_DIM, (a + 1) * FOX_AUG_DIM)
            k = k_ref[0, kb * tk:(kb + 1) * tk, lanes]
            q = q_ref[0, i * tq:(i + 1) * tq, lanes]
            st_ref[t % slots, a] = lax.dot_general(k, q, (((1,), (1,)), ((), ())),
                                                   preferred_element_type=F32)

    sum_rows = 16
    rid = lax.broadcasted_iota(jnp.int32, (sum_rows, tk), 0)
    ones_rows = jnp.where(rid == 0, 1.0, 0.0).astype(BF16)

    def update(a, kb, st, stats, masked):
        m, acc = stats
        if masked:
            kv = lax.broadcasted_iota(jnp.int32, st.shape, 0)
            qq = lax.broadcasted_iota(jnp.int32, st.shape, 1)
            st = jnp.where(kv <= qq, st, NEG_INF)
        m_new = jnp.maximum(m, jnp.max(st, axis=0, keepdims=True))
        alpha = jnp.exp2(m - m_new)
        p = jnp.exp2(st - m_new)
        vt = vt_ref[0, kb, a * FOX_HEAD_DIM:(a + 1) * FOX_HEAD_DIM, :]
        vt = jnp.concatenate([vt, ones_rows], axis=0)
        return m_new, alpha * acc + _dot(vt, p.astype(BF16))

    def init():
        return [(jnp.full((1, tq), NEG_INF, F32), jnp.zeros((FOX_HEAD_DIM + sum_rows, tq), F32))
                for _ in heads]

    for t in range(slots - 1):
        issue_scores(t)
    stats = init()
    for t, (i, kb) in enumerate(steps):
        if t + slots - 1 < len(steps):
            issue_scores(t + slots - 1)
        stats = [update(a, kb, st_ref[t % slots, a], stats[a], masked=(kb == i)) for a in heads]
        if kb == i:
            outs = [acc[:FOX_HEAD_DIM] / acc[FOX_HEAD_DIM:FOX_HEAD_DIM + 1] for _, acc in stats]
            o = jnp.concatenate(outs, axis=0).T
            rows = slice(i * tq, (i + 1) * tq)
            o_ref[0, rows, :] = (o * _silu(z_ref[0, rows, :].astype(F32))).astype(BF16)
            stats = init()


def _fox_attn(qa, ka, vt, z):
    bsz, seq, _ = z.shape
    tq = ATTN_Q_ROWS
    tk = ATTN_KV_ROWS
    assert tq == tk and vt.shape == (bsz, seq // tk, FOX_WIDTH, tk)
    pair = 2 * FOX_HEAD_DIM
    seq_spec = lambda w: pl.BlockSpec((1, seq, w), lambda b, j: (b, 0, j))
    return pl.pallas_call(
        functools.partial(_fox_attn_kernel, tq=tq, tk=tk),
        grid=(bsz, FOX_HEADS // 2),
        in_specs=[seq_spec(2 * FOX_AUG_DIM), seq_spec(2 * FOX_AUG_DIM),
                  pl.BlockSpec((1, seq // tk, pair, tk), lambda b, j: (b, 0, j, 0)),
                  seq_spec(pair)],
        out_specs=seq_spec(pair),
        out_shape=jax.ShapeDtypeStruct((bsz, seq, FOX_WIDTH), BF16),
        scratch_shapes=[pltpu.VMEM((ATTN_SCORE_SLOTS, 2, tk, tq), F32)],
        compiler_params=_params("parallel", "parallel"),
        name="fox_attn",
    )(qa, ka, vt, z)


def _s5_proj_kernel(x_ref, g_ref, wu_ref, wz_ref, u_out, z_out):
    h = _rmsnorm(x_ref[0], g_ref[...]).astype(BF16)
    u_out[0] = _dot(h, wu_ref[...]).astype(BF16)
    z_out[0] = _dot(h, wz_ref[...]).astype(BF16)


def _s5_proj(x, g, w_in):
    bsz, seq, d = x.shape
    wu = w_in[:, :d].astype(BF16)
    wz = w_in[:, d:].astype(BF16)
    t = PROJ_ROWS
    row_spec = pl.BlockSpec((1, t, d), lambda b, i: (b, i, 0))
    return pl.pallas_call(
        _s5_proj_kernel,
        grid=(bsz, seq // t),
        in_specs=[row_spec, _const_spec((1, d)), _const_spec(wu.shape), _const_spec(wz.shape)],
        out_specs=[row_spec, row_spec],
        out_shape=[jax.ShapeDtypeStruct((bsz, seq, d), BF16)] * 2,
        compiler_params=_params("parallel", "parallel"),
        name="s5_proj",
    )(x, g.reshape(1, d), wu, wz)


def _s5_group_kernel(x_ref, w_ref, bre_ref, bim_ref, cre_ref, cim_ref, ar_ref, ai_ref, d_ref,
                     y_ref, sre_ref, sim_ref, pre_ref, pim_ref, *, batch, chunks):
    x = x_ref[0]
    sre_ref[...] = _dot(x, bre_ref[0])
    sim_ref[...] = _dot(x, bim_ref[0])
    ar = ar_ref[0]
    ai = ai_ref[0]

    def body(c, carry):
        xr, xi = carry
        rows = pl.ds(pl.multiple_of(c * batch, batch), batch)
        pre_ref[rows, :] = xr
        pim_ref[rows, :] = xi
        nxr = ar * xr - ai * xi + sre_ref[rows, :]
        nxi = ar * xi + ai * xr + sim_ref[rows, :]
        return nxr, nxi

    zero = jnp.zeros((batch, S5_STATE), F32)
    lax.fori_loop(0, chunks, body, (zero, zero))

    y = _dot(x, w_ref[0])
    y += _dot(pre_ref[...].astype(BF16), cre_ref[0])
    y += _dot(pim_ref[...].astype(BF16), cim_ref[0])
    y_ref[0] = y + d_ref[0] * x.astype(F32)


def _s5_operators(lam_re, lam_im, log_dt, b_re, b_im, c_re, c_im, d_skip):
    hp = lax.Precision.HIGHEST
    t = S5_CHUNK
    dt = jnp.exp(log_dt)[:, None]
    tau = jnp.arange(t + 1, dtype=F32)[:, None, None]
    mag = jnp.exp(lam_re * dt * tau)
    ang = lam_im * dt * tau
    pw_re = mag * jnp.cos(ang)
    pw_im = mag * jnp.sin(ang)
    lb_re, lb_im = pw_re[1], pw_im[1]
    den = lam_re * lam_re + lam_im * lam_im
    num_re = lb_re - 1.0
    num_im = lb_im
    coef_re = (num_re * lam_re + num_im * lam_im) / den
    coef_im = (num_im * lam_re - num_re * lam_im) / den
    bb_re = coef_re[..., None] * b_re - coef_im[..., None] * b_im
    bb_im = coef_re[..., None] * b_im + coef_im[..., None] * b_re
    m_re = pw_re[..., None] * bb_re - pw_im[..., None] * bb_im
    m_im = pw_re[..., None] * bb_im + pw_im[..., None] * bb_re
    kern = (jnp.einsum('gop,tgpi->tgoi', c_re, m_re[:t], precision=hp)
            - jnp.einsum('gop,tgpi->tgoi', c_im, m_im[:t], precision=hp))
    lag = jnp.arange(t)[None, :] - jnp.arange(t)[:, None]
    toep = jnp.where((lag >= 0)[:, :, None, None, None], kern[jnp.maximum(lag, 0)], 0.0)
    w = toep.transpose(2, 0, 4, 1, 3).reshape(S5_GROUPS, S5_ROW, S5_ROW)
    bst_re = m_re[:t][::-1].transpose(1, 0, 3, 2).reshape(S5_GROUPS, S5_ROW, S5_STATE)
    bst_im = m_im[:t][::-1].transpose(1, 0, 3, 2).reshape(S5_GROUPS, S5_ROW, S5_STATE)
    pr = pw_re[1:].transpose(1, 2, 0)[:, :, :, None]
    pi = pw_im[1:].transpose(1, 2, 0)[:, :, :, None]
    cr = c_re.transpose(0, 2, 1)[:, :, None, :]
    ci = c_im.transpose(0, 2, 1)[:, :, None, :]
    cst_re = (cr * pr - ci * pi).reshape(S5_GROUPS, S5_STATE, S5_ROW)
    cst_im = (-(cr * pi + ci * pr)).reshape(S5_GROUPS, S5_STATE, S5_ROW)
    a_re = pw_re[t].reshape(S5_GROUPS, 1, S5_STATE)
    a_im = pw_im[t].reshape(S5_GROUPS, 1, S5_STATE)
    d_row = jnp.tile(d_skip.reshape(S5_GROUPS, 1, S5_GROUP), (1, 1, t))
    return (w.astype(BF16), bst_re.astype(BF16), bst_im.astype(BF16), cst_re.astype(BF16),
            cst_im.astype(BF16), a_re, a_im, d_row)


def _s5_ssm(u, ops):
    bsz, seq, _ = u.shape
    chunks = seq // S5_CHUNK
    rows = chunks * bsz
    xg = u.reshape(bsz, chunks, S5_CHUNK, S5_GROUPS, S5_GROUP).transpose(3, 1, 0, 2, 4)
    xg = xg.reshape(S5_GROUPS, rows, S5_ROW)
    g_spec = lambda a, b: pl.BlockSpec((1, a, b), lambda g: (g, 0, 0))
    yg = pl.pallas_call(
        functools.partial(_s5_group_kernel, batch=bsz, chunks=chunks),
        grid=(S5_GROUPS,),
        in_specs=[g_spec(rows, S5_ROW), g_spec(S5_ROW, S5_ROW),
                  g_spec(S5_ROW, S5_STATE), g_spec(S5_ROW, S5_STATE),
                  g_spec(S5_STATE, S5_ROW), g_spec(S5_STATE, S5_ROW),
                  g_spec(1, S5_STATE), g_spec(1, S5_STATE), g_spec(1, S5_ROW)],
        out_specs=g_spec(rows, S5_ROW),
        out_shape=jax.ShapeDtypeStruct((S5_GROUPS, rows, S5_ROW), F32),
        scratch_shapes=[pltpu.VMEM((rows, S5_STATE), F32)] * 4,
        compiler_params=_params("parallel"),
        name="s5_ssm",
    )(xg, *ops)
    y = yg.reshape(S5_GROUPS, chunks, bsz, S5_CHUNK, S5_GROUP).transpose(2, 1, 3, 0, 4)
    return y.reshape(bsz, seq, S5_GROUPS * S5_GROUP)


def _post_kernel(*refs, glu):
    if glu:
        (y_ref, z_ref, wglu_ref, bglu_ref, x_ref, p_ref, wo_ref, g_ref, wg_ref, bg_ref, wp_ref,
         o_ref) = refs
        gl = _gelu_tanh(y_ref[0])
        gl = gl * _sigmoid(_dot(gl.astype(BF16), wglu_ref[...]) + bglu_ref[...])
        mix = (gl * _silu(z_ref[0].astype(F32))).astype(BF16)
    else:
        a_ref, x_ref, p_ref, wo_ref, g_ref, wg_ref, bg_ref, wp_ref, o_ref = refs
        mix = a_ref[0]
    y = _dot(mix, wo_ref[...])
    x1 = x_ref[0] + _rmsnorm(y, g_ref[...])
    gate = _sigmoid(_dot(x1.astype(BF16), wg_ref[...]) + bg_ref[...])
    o_ref[0] = x1 + _dot(p_ref[0, 0].astype(BF16), wp_ref[...]) * gate


def _post(mix_inputs, x, p, layer, w_out, g_post, w_gate, b_gate, w_proj, glu_params=None):
    bsz, seq, d = x.shape
    t = POST_ROWS
    row_spec = pl.BlockSpec((1, t, d), lambda b, i: (b, i, 0))
    p_spec = pl.BlockSpec((1, 1, t, PLE_DIM), lambda b, i: (layer, b, i, 0))
    glu = glu_params is not None
    args = list(mix_inputs)
    specs = [row_spec] * len(args)
    if glu:
        w_glu, b_glu = glu_params
        args += [w_glu.astype(BF16), b_glu.reshape(1, d)]
        specs += [_const_spec((d, d)), _const_spec((1, d))]
    args += [x, p, w_out.astype(BF16), g_post.reshape(1, d), w_gate.astype(BF16),
             b_gate.reshape(1, d), w_proj.astype(BF16)]
    specs += [row_spec, p_spec, _const_spec((d, d)), _const_spec((1, d)), _const_spec((d, d)),
              _const_spec((1, d)), _const_spec((PLE_DIM, d))]
    return pl.pallas_call(
        functools.partial(_post_kernel, glu=glu),
        grid=(bsz, seq // t),
        in_specs=specs,
        out_specs=row_spec,
        out_shape=jax.ShapeDtypeStruct((bsz, seq, d), F32),
        compiler_params=_params("parallel", "parallel"),
        name="s5_post" if glu else "fox_post",
    )(*args)


def kernel(x, p, ln_pre, ln_post, fox_w_in, fox_b_f, fox_w_out, s5_w_in, s5_lambda_re, s5_lambda_im,
           s5_log_dt, s5_b_re, s5_b_im, s5_c_re, s5_c_im, s5_d, s5_w_glu, s5_b_glu, s5_w_out,
           ple_w_proj, ple_w_gate, ple_b_gate):
    qa, ka, vt, z = _fox_proj(x, ln_pre[0], fox_w_in[0], fox_b_f[0])
    gated = _fox_attn(qa, ka, vt, z)
    x = _post([gated], x, p, 0, fox_w_out[0], ln_post[0], ple_w_gate[0], ple_b_gate[0],
              ple_w_proj[0])
    u, z = _s5_proj(x, ln_pre[1], s5_w_in[0])
    ops = _s5_operators(s5_lambda_re[0], s5_lambda_im[0], s5_log_dt[0], s5_b_re[0], s5_b_im[0],
                        s5_c_re[0], s5_c_im[0], s5_d[0])
    y = _s5_ssm(u, ops)
    x = _post([y, z], x, p, 1, s5_w_out[0], ln_post[1], ple_w_gate[1], ple_b_gate[1],
              ple_w_proj[1], glu_params=(s5_w_glu[0], s5_b_glu[0]))
    return x
```

```python
import functools
import math

import jax
import jax.numpy as jnp
import numpy as np
from jax import lax
from jax.experimental import pallas as pl
from jax.experimental.pallas import tpu as pltpu

F32 = jnp.float32
BF16 = jnp.bfloat16

D_MODEL = 1024
PLE_DIM = 256
RMS_EPS = 1e-6
NEG_INF = -1e30

FOX_HEADS = 16
FOX_HEAD_DIM = 64
FOX_WIDTH = FOX_HEADS * FOX_HEAD_DIM
FOX_AUG_DIM = 128
FOX_AUG_WIDTH = FOX_HEADS * FOX_AUG_DIM

S5_GROUP = 16
S5_GROUPS = D_MODEL // S5_GROUP
S5_STATE = 64
S5_CHUNK = 16
S5_ROW = S5_CHUNK * S5_GROUP

V7X_SCOPED_VMEM_LIMIT_BYTES = 56 * 1024 * 1024

PROJ_ROWS = 256
POST_ROWS = 512
ATTN_Q_ROWS = 256
ATTN_KV_ROWS = PROJ_ROWS
ATTN_SCORE_SLOTS = 3
LOG2_E = 1.4426950408889634


def _const_spec(shape):
    zeros = (0,) * len(shape)
    return pl.BlockSpec(shape, lambda *_: zeros, pipeline_mode=pl.Buffered(1))


def _params(*semantics):
    return pltpu.CompilerParams(dimension_semantics=semantics,
                                vmem_limit_bytes=V7X_SCOPED_VMEM_LIMIT_BYTES)


def _dot(a, b):
    return jnp.dot(a, b, preferred_element_type=F32)


def _rmsnorm(x, g):
    var = jnp.mean(x * x, axis=-1, keepdims=True)
    return x * lax.rsqrt(var + RMS_EPS) * g


def _sigmoid(x):
    return 1.0 / (1.0 + jnp.exp(-x))


def _silu(x):
    return x * _sigmoid(x)


def _gelu_tanh(x):
    c = math.sqrt(2.0 / math.pi)
    return 0.5 * x * (1.0 + jnp.tanh(c * (x + 0.044715 * (x * x * x))))


def _split3(x):
    hi = x.astype(BF16).astype(F32)
    r1 = x - hi
    mid = r1.astype(BF16).astype(F32)
    lo = (r1 - mid).astype(BF16).astype(F32)
    return hi, mid, lo


def _fox_proj_kernel(x_ref, g_ref, wq_ref, wk_ref, wv_ref, wz_ref, wf_ref, bf_ref, eq_ref, ek_ref,
                     q_out, k_out, v_out, z_out, carry_ref):
    rows = x_ref.shape[1]

    @pl.when(pl.program_id(1) == 0)
    def _():
        carry_ref[...] = jnp.zeros_like(carry_ref)

    h = _rmsnorm(x_ref[0], g_ref[...]).astype(BF16)

    zf = _dot(h, wf_ref[...]) + bf_ref[...]
    log_f = jnp.minimum(zf, 0.0) - jnp.log(1.0 + jnp.exp(-jnp.abs(zf)))
    r = lax.broadcasted_iota(jnp.int32, (rows, rows), 0)
    c = lax.broadcasted_iota(jnp.int32, (rows, rows), 1)
    tri = jnp.where(r >= c, 1.0, 0.0).astype(BF16)
    hi, mid, lo = _split3(log_f)
    cum = (_dot(tri, hi.astype(BF16)) + _dot(tri, mid.astype(BF16))
           + _dot(tri, lo.astype(BF16))) + carry_ref[0:1, :]
    carry_ref[...] = jnp.broadcast_to(cum[rows - 1:rows, :], carry_ref.shape)

    c_hi, c_mid, c_lo = _split3(cum * LOG2_E)
    lane = lax.broadcasted_iota(jnp.int32, cum.shape, 1)
    aug = jnp.where(lane < 16, c_hi,
                    jnp.where(lane < 32, c_mid,
                              jnp.where(lane < 48, c_lo,
                                        jnp.where(lane == 48, 1.0, 0.0)))).astype(BF16)

    def widen(qk, placed):
        low = lax.broadcasted_iota(jnp.int32, (rows, FOX_AUG_DIM), 1) < FOX_HEAD_DIM
        tiles = []
        for j in range(FOX_HEADS // 2):
            pair = qk[:, j * FOX_AUG_DIM:(j + 1) * FOX_AUG_DIM]
            for a, val in enumerate((pair, pltpu.roll(pair, FOX_HEAD_DIM, 1))):
                lanes = slice((2 * j + a) * FOX_AUG_DIM, (2 * j + a + 1) * FOX_AUG_DIM)
                tiles.append(jnp.where(low, val, placed[:, lanes]))
        return jnp.concatenate(tiles, axis=1).astype(BF16)

    q_out[0] = widen(_dot(h, wq_ref[...]) * LOG2_E, _dot(aug, eq_ref[...]))
    k_out[0] = widen(_dot(h, wk_ref[...]), _dot(aug, ek_ref[...]))
    v_out[0, 0] = lax.dot_general(wv_ref[...], h, (((1,), (1,)), ((), ())),
                                  preferred_element_type=F32).astype(BF16)
    z_out[0] = _dot(h, wz_ref[...]).astype(BF16)


def _fox_aug_placement():
    eq = np.zeros((128, FOX_AUG_WIDTH), np.float32)
    ek = np.zeros((128, FOX_AUG_WIDTH), np.float32)
    for h in range(FOX_HEADS):
        base = h * FOX_AUG_DIM + FOX_HEAD_DIM
        for part in range(3):
            eq[part * 16 + h, base + part] = 1.0
            ek[part * 16 + h, base + 3 + part] = -1.0
            eq[48, base + 3 + part] = 1.0
            ek[48, base + part] = 1.0
    return jnp.asarray(eq, BF16), jnp.asarray(ek, BF16)


def _fox_proj(x, g, w_in, b_f):
    bsz, seq, d = x.shape
    scale = FOX_HEAD_DIM ** -0.5
    wq = (w_in[:, :FOX_WIDTH] * scale).astype(BF16)
    wk = w_in[:, FOX_WIDTH:2 * FOX_WIDTH].astype(BF16)
    wv = w_in[:, 2 * FOX_WIDTH:3 * FOX_WIDTH].T.astype(BF16)
    wz = w_in[:, 3 * FOX_WIDTH:4 * FOX_WIDTH].astype(BF16)
    wf = w_in[:, 4 * FOX_WIDTH:]
    wf = jnp.pad(jnp.tile(wf, (1, 3)), ((0, 0), (0, 128 - 3 * FOX_HEADS))).astype(BF16)
    bf = jnp.pad(jnp.tile(b_f, 3), (0, 128 - 3 * FOX_HEADS)).reshape(1, 128).astype(F32)
    eq, ek = _fox_aug_placement()

    t = PROJ_ROWS
    row_spec = lambda w: pl.BlockSpec((1, t, w), lambda b, i: (b, i, 0))
    return pl.pallas_call(
        _fox_proj_kernel,
        grid=(bsz, seq // t),
        in_specs=[row_spec(d), _const_spec((1, d)),
                  _const_spec(wq.shape), _const_spec(wk.shape), _const_spec(wv.shape),
                  _const_spec(wz.shape), _const_spec(wf.shape), _const_spec(bf.shape),
                  _const_spec(eq.shape), _const_spec(ek.shape)],
        out_specs=[row_spec(FOX_AUG_WIDTH), row_spec(FOX_AUG_WIDTH),
                   pl.BlockSpec((1, 1, FOX_WIDTH, t), lambda b, i: (b, i, 0, 0)),
                   row_spec(FOX_WIDTH)],
        out_shape=[jax.ShapeDtypeStruct((bsz, seq, FOX_AUG_WIDTH), BF16),
                   jax.ShapeDtypeStruct((bsz, seq, FOX_AUG_WIDTH), BF16),
                   jax.ShapeDtypeStruct((bsz, seq // t, FOX_WIDTH, t), BF16),
                   jax.ShapeDtypeStruct((bsz, seq, FOX_WIDTH), BF16)],
        scratch_shapes=[pltpu.VMEM((8, 128), F32)],
        compiler_params=_params("parallel", "arbitrary"),
        name="fox_proj",
    )(x, g.reshape(1, d), wq, wk, wv, wz, wf, bf, eq, ek)


def _fox_attn_kernel(q_ref, k_ref, vt_ref, z_ref, o_ref, st_ref, *, tq, tk):
    seq = q_ref.shape[1]
    heads = (0, 1)
    slots = st_ref.shape[0]
    steps = [(i, kb) for i in range(seq // tq) for kb in range(i + 1)]

    def issue_scores(t):
        i, kb = steps[t]
        for a in heads:
            lanes = slice(a * FOX_AUG_DIM, (a + 1) * FOX_AUG_DIM)
            k = k_ref[0, kb * tk:(kb + 1) * tk, lanes]
            q = q_ref[0, i * tq:(i + 1) * tq, lanes]
            st_ref[t % slots, a] = lax.dot_general(k, q, (((1,), (1,)), ((), ())),
                                                   preferred_element_type=F32)

    sum_rows = 16
    rid = lax.broadcasted_iota(jnp.int32, (sum_rows, tk), 0)
    ones_rows = jnp.where(rid == 0, 1.0, 0.0).astype(BF16)

    def update(a, kb, st, stats, masked):
        m, acc = stats
        if masked:
            kv = lax.broadcasted_iota(jnp.int32, st.shape, 0)
            qq = lax.broadcasted_iota(jnp.int32, st.shape, 1)
            st = jnp.where(kv <= qq, st, NEG_INF)
        m_new = jnp.maximum(m, jnp.max(st, axis=0, keepdims=True))
        alpha = jnp.exp2(m - m_new)
        p = jnp.exp2(st - m_new)
        vt = vt_ref[0, kb, a * FOX_HEAD_DIM:(a + 1) * FOX_HEAD_DIM, :]
        vt = jnp.concatenate([vt, ones_rows], axis=0)
        return m_new, alpha * acc + _dot(vt, p.astype(BF16))

    def init():
        return [(jnp.full((1, tq), NEG_INF, F32), jnp.zeros((FOX_HEAD_DIM + sum_rows, tq), F32))
                for _ in heads]

    for t in range(slots - 1):
        issue_scores(t)
    stats = init()
    for t, (i, kb) in enumerate(steps):
        if t + slots - 1 < len(steps):
            issue_scores(t + slots - 1)
        stats = [update(a, kb, st_ref[t % slots, a], stats[a], masked=(kb == i)) for a in heads]
        if kb == i:
            outs = [acc[:FOX_HEAD_DIM] / acc[FOX_HEAD_DIM:FOX_HEAD_DIM + 1] for _, acc in stats]
            o = jnp.concatenate(outs, axis=0).T
            rows = slice(i * tq, (i + 1) * tq)
            o_ref[0, rows, :] = (o * _silu(z_ref[0, rows, :].astype(F32))).astype(BF16)
            stats = init()


def _fox_attn(qa, ka, vt, z):
    bsz, seq, _ = z.shape
    tq = ATTN_Q_ROWS
    tk = ATTN_KV_ROWS
    assert tq == tk and vt.shape == (bsz, seq // tk, FOX_WIDTH, tk)
    pair = 2 * FOX_HEAD_DIM
    seq_spec = lambda w: pl.BlockSpec((1, seq, w), lambda b, j: (b, 0, j))
    return pl.pallas_call(
        functools.partial(_fox_attn_kernel, tq=tq, tk=tk),
        grid=(bsz, FOX_HEADS // 2),
        in_specs=[seq_spec(2 * FOX_AUG_DIM), seq_spec(2 * FOX_AUG_DIM),
                  pl.BlockSpec((1, seq // tk, pair, tk), lambda b, j: (b, 0, j, 0)),
                  seq_spec(pair)],
        out_specs=seq_spec(pair),
        out_shape=jax.ShapeDtypeStruct((bsz, seq, FOX_WIDTH), BF16),
        scratch_shapes=[pltpu.VMEM((ATTN_SCORE_SLOTS, 2, tk, tq), F32)],
        compiler_params=_params("parallel", "parallel"),
        name="fox_attn",
    )(qa, ka, vt, z)


def _stack_tokens(ref, width):
    return jnp.concatenate([ref[:, s * width:(s + 1) * width] for s in range(S5_CHUNK)], axis=0)


def _unstack_tokens(val, ref, width):
    batch = val.shape[0] // S5_CHUNK
    for s in range(S5_CHUNK):
        ref[:, s * width:(s + 1) * width] = val[s * batch:(s + 1) * batch].astype(ref.dtype)


def _slot_masks(batch):
    slot = lax.broadcasted_iota(jnp.int32, (batch, 128), 1) >> 4
    return [slot == q for q in range(128 // S5_GROUP)]


def _s5_front_kernel(x_ref, g_ref, wu_ref, wz_ref, xg_out, z_out):
    batch = x_ref.shape[0]
    d = g_ref.shape[1]
    slots = 128 // S5_GROUP
    h = _rmsnorm(_stack_tokens(x_ref, d), g_ref[...]).astype(BF16)
    _unstack_tokens(_dot(h, wz_ref[...]), z_out, d)
    u = _dot(h, wu_ref[...])
    masks = _slot_masks(batch)
    rolled = []
    for s in range(S5_CHUNK):
        shift = (s % slots) * S5_GROUP
        tiles = [u[s * batch:(s + 1) * batch, j * 128:(j + 1) * 128] for j in range(d // 128)]
        rolled.append([pltpu.roll(t, shift, 1) if shift else t for t in tiles])
    for g in range(S5_GROUPS):
        j, r = divmod(g, slots)
        halves = []
        for k in range(S5_CHUNK // slots):
            acc = rolled[slots * k + (-r) % slots][j]
            for q in range(1, slots):
                acc = jnp.where(masks[q], rolled[slots * k + (q - r) % slots][j], acc)
            halves.append(acc)
        xg_out[g, 0] = jnp.concatenate(halves, axis=1).astype(BF16)


def _s5_front(x, g, w_in):
    bsz, seq, d = x.shape
    chunks = seq // S5_CHUNK
    wu = w_in[:, :d].astype(BF16)
    wz = w_in[:, d:].astype(BF16)
    chunk_spec = pl.BlockSpec((bsz, S5_CHUNK * d), lambda c: (0, c))
    return pl.pallas_call(
        _s5_front_kernel,
        grid=(chunks,),
        in_specs=[chunk_spec, _const_spec((1, d)), _const_spec(wu.shape), _const_spec(wz.shape)],
        out_specs=[pl.BlockSpec((S5_GROUPS, 1, bsz, S5_ROW), lambda c: (0, c, 0, 0)), chunk_spec],
        out_shape=[jax.ShapeDtypeStruct((S5_GROUPS, chunks, bsz, S5_ROW), BF16),
                   jax.ShapeDtypeStruct((bsz, seq * d), BF16)],
        compiler_params=_params("parallel"),
        name="s5_front",
    )(x.reshape(bsz, seq * d), g.reshape(1, d), wu, wz)


def _s5_group_kernel(x_ref, w_ref, bst_ref, cst_ref, ar_ref, ai_ref, y_ref,
                     sre_ref, sim_ref, pre_ref, pim_ref, *, batch, chunks):
    x = x_ref[0]
    s = _dot(x, bst_ref[0])
    sre_ref[...] = s[:, :S5_STATE]
    sim_ref[...] = s[:, S5_STATE:]
    ar = ar_ref[0]
    ai = ai_ref[0]

    def body(c, carry):
        xr, xi = carry
        rows = pl.ds(pl.multiple_of(c * batch, batch), batch)
        pre_ref[rows, :] = xr
        pim_ref[rows, :] = xi
        nxr = ar * xr - ai * xi + sre_ref[rows, :]
        nxi = ar * xi + ai * xr + sim_ref[rows, :]
        return nxr, nxi

    zero = jnp.zeros((batch, S5_STATE), F32)
    lax.fori_loop(0, chunks, body, (zero, zero))
    prev = jnp.concatenate([pre_ref[...], pim_ref[...]], axis=1).astype(BF16)
    y_ref[0] = _dot(x, w_ref[0]) + _dot(prev, cst_ref[0])


def _s5_operators(lam_re, lam_im, log_dt, b_re, b_im, c_re, c_im, d_skip):
    hp = lax.Precision.HIGHEST
    t = S5_CHUNK
    dt = jnp.exp(log_dt)[:, None]
    tau = jnp.arange(t + 1, dtype=F32)[:, None, None]
    mag = jnp.exp(lam_re * dt * tau)
    ang = lam_im * dt * tau
    pw_re = mag * jnp.cos(ang)
    pw_im = mag * jnp.sin(ang)
    lb_re, lb_im = pw_re[1], pw_im[1]
    den = lam_re * lam_re + lam_im * lam_im
    num_re = lb_re - 1.0
    num_im = lb_im
    coef_re = (num_re * lam_re + num_im * lam_im) / den
    coef_im = (num_im * lam_re - num_re * lam_im) / den
    bb_re = coef_re[..., None] * b_re - coef_im[..., None] * b_im
    bb_im = coef_re[..., None] * b_im + coef_im[..., None] * b_re
    m_re = pw_re[..., None] * bb_re - pw_im[..., None] * bb_im
    m_im = pw_re[..., None] * bb_im + pw_im[..., None] * bb_re
    kern = (jnp.einsum('gop,tgpi->tgoi', c_re, m_re[:t], precision=hp)
            - jnp.einsum('gop,tgpi->tgoi', c_im, m_im[:t], precision=hp))
    skip = d_skip.reshape(S5_GROUPS, S5_GROUP)[:, :, None] * jnp.eye(S5_GROUP, dtype=F32)
    kern = kern.at[0].add(skip)
    lag = jnp.arange(t)[None, :] - jnp.arange(t)[:, None]
    toep = jnp.where((lag >= 0)[:, :, None, None, None], kern[jnp.maximum(lag, 0)], 0.0)
    w = toep.transpose(2, 0, 4, 1, 3).reshape(S5_GROUPS, S5_ROW, S5_ROW)
    bst_re = m_re[:t][::-1].transpose(1, 0, 3, 2).reshape(S5_GROUPS, S5_ROW, S5_STATE)
    bst_im = m_im[:t][::-1].transpose(1, 0, 3, 2).reshape(S5_GROUPS, S5_ROW, S5_STATE)
    bst = jnp.concatenate([bst_re, bst_im], axis=2)
    pr = pw_re[1:].transpose(1, 2, 0)[:, :, :, None]
    pi = pw_im[1:].transpose(1, 2, 0)[:, :, :, None]
    cr = c_re.transpose(0, 2, 1)[:, :, None, :]
    ci = c_im.transpose(0, 2, 1)[:, :, None, :]
    cst_re = (cr * pr - ci * pi).reshape(S5_GROUPS, S5_STATE, S5_ROW)
    cst_im = (-(cr * pi + ci * pr)).reshape(S5_GROUPS, S5_STATE, S5_ROW)
    cst = jnp.concatenate([cst_re, cst_im], axis=1)
    n = np.arange(S5_ROW)
    k, q, i = n // 128, (n % 128) // S5_GROUP, n % S5_GROUP
    token = 8 * k + (q - np.arange(S5_GROUPS)[:, None] % 8) % 8
    rows = jnp.asarray(token * S5_GROUP + i)[:, :, None]
    w = jnp.take_along_axis(w, rows, axis=1)
    bst = jnp.take_along_axis(bst, rows, axis=1)
    a_re = pw_re[t].reshape(S5_GROUPS, 1, S5_STATE)
    a_im = pw_im[t].reshape(S5_GROUPS, 1, S5_STATE)
    return w.astype(BF16), bst.astype(BF16), cst.astype(BF16), a_re, a_im


def _s5_ssm(xg, ops):
    _, chunks, bsz, _ = xg.shape
    rows = chunks * bsz
    g_spec = lambda a, b: pl.BlockSpec((1, a, b), lambda g: (g, 0, 0))
    yg = pl.pallas_call(
        functools.partial(_s5_group_kernel, batch=bsz, chunks=chunks),
        grid=(S5_GROUPS,),
        in_specs=[g_spec(rows, S5_ROW), g_spec(S5_ROW, S5_ROW), g_spec(S5_ROW, 2 * S5_STATE),
                  g_spec(2 * S5_STATE, S5_ROW), g_spec(1, S5_STATE), g_spec(1, S5_STATE)],
        out_specs=g_spec(rows, S5_ROW),
        out_shape=jax.ShapeDtypeStruct((S5_GROUPS, rows, S5_ROW), F32),
        scratch_shapes=[pltpu.VMEM((rows, S5_STATE), F32)] * 4,
        compiler_params=_params("parallel"),
        name="s5_ssm",
    )(xg.reshape(S5_GROUPS, rows, S5_ROW), *ops)
    return yg.reshape(S5_GROUPS, chunks, bsz, S5_ROW)


def _residual_ple(mix, x, p, wo_ref, g_ref, wg_ref, bg_ref, wp_ref):
    x1 = x + _rmsnorm(_dot(mix, wo_ref[...]), g_ref[...])
    gate = _sigmoid(_dot(x1.astype(BF16), wg_ref[...]) + bg_ref[...])
    return x1 + _dot(p.astype(BF16), wp_ref[...]) * gate


def _fox_post_kernel(a_ref, x_ref, p_ref, wo_ref, g_ref, wg_ref, bg_ref, wp_ref, o_ref):
    o_ref[0] = _residual_ple(a_ref[0], x_ref[0], p_ref[0, 0], wo_ref, g_ref, wg_ref, bg_ref, wp_ref)


def _s5_post_kernel(yg_ref, z_ref, x_ref, p_ref, wglu_ref, bglu_ref, wo_ref, g_ref, wg_ref, bg_ref,
                    wp_ref, o_ref):
    batch = x_ref.shape[0]
    d = g_ref.shape[1]
    slots = 128 // S5_GROUP
    slot = lax.broadcasted_iota(jnp.int32, (batch, 128), 1) >> 4
    tok = [[None] * (d // 128) for _ in range(S5_CHUNK)]
    for j in range(d // 128):
        for k in range(S5_CHUNK // slots):
            tiles = [yg_ref[slots * j + r, 0, :, k * 128:(k + 1) * 128] for r in range(slots)]
            for dist in (4, 2, 1):
                low = (slot & (2 * dist - 1)) < dist
                nxt = list(tiles)
                for r in range(slots):
                    if not r & dist:
                        a, b = tiles[r], tiles[r + dist]
                        nxt[r] = jnp.where(low, a, pltpu.roll(b, dist * S5_GROUP, 1))
                        nxt[r + dist] = jnp.where(low, pltpu.roll(a, 128 - dist * S5_GROUP, 1), b)
                tiles = nxt
            for tq in range(slots):
                tok[slots * k + tq][j] = tiles[tq]
    y = jnp.concatenate([jnp.concatenate(t, axis=1) for t in tok], axis=0)
    gl = _gelu_tanh(y)
    gl = gl * _sigmoid(_dot(gl.astype(BF16), wglu_ref[...]) + bglu_ref[...])
    mix = (gl * _silu(_stack_tokens(z_ref, d).astype(F32))).astype(BF16)
    out = _residual_ple(mix, _stack_tokens(x_ref, d), _stack_tokens(p_ref.at[0], PLE_DIM),
                        wo_ref, g_ref, wg_ref, bg_ref, wp_ref)
    _unstack_tokens(out, o_ref, d)


def _tail_weights(w_out, g_post, w_gate, b_gate, w_proj):
    d = w_out.shape[0]
    args = [w_out.astype(BF16), g_post.reshape(1, d), w_gate.astype(BF16), b_gate.reshape(1, d),
            w_proj.astype(BF16)]
    return args, [_const_spec(a.shape) for a in args]


def _fox_post(gated, x, p, layer, *tail):
    bsz, seq, d = x.shape
    t = POST_ROWS
    row_spec = pl.BlockSpec((1, t, d), lambda b, i: (b, i, 0))
    p_spec = pl.BlockSpec((1, 1, t, PLE_DIM), lambda b, i: (layer, b, i, 0))
    w_args, w_specs = _tail_weights(*tail)
    return pl.pallas_call(
        _fox_post_kernel,
        grid=(bsz, seq // t),
        in_specs=[row_spec, row_spec, p_spec] + w_specs,
        out_specs=row_spec,
        out_shape=jax.ShapeDtypeStruct((bsz, seq, d), F32),
        compiler_params=_params("parallel", "parallel"),
        name="fox_post",
    )(gated, x, p, *w_args)


def _s5_post(yg, z2d, x, p, layer, w_glu, b_glu, *tail):
    bsz, seq, d = x.shape
    chunks = seq // S5_CHUNK
    chunk_spec = pl.BlockSpec((bsz, S5_CHUNK * d), lambda c: (0, c))
    p_spec = pl.BlockSpec((1, bsz, S5_CHUNK * PLE_DIM), lambda c: (layer, 0, c))
    w_args, w_specs = _tail_weights(*tail)
    out = pl.pallas_call(
        _s5_post_kernel,
        grid=(chunks,),
        in_specs=[pl.BlockSpec((S5_GROUPS, 1, bsz, S5_ROW), lambda c: (0, c, 0, 0)), chunk_spec,
                  chunk_spec, p_spec, _const_spec((d, d)), _const_spec((1, d))] + w_specs,
        out_specs=chunk_spec,
        out_shape=jax.ShapeDtypeStruct((bsz, seq * d), F32),
        compiler_params=_params("parallel"),
        name="s5_post",
    )(yg, z2d, x.reshape(bsz, seq * d), p.reshape(p.shape[0], bsz, seq * PLE_DIM),
      w_glu.astype(BF16), b_glu.reshape(1, d), *w_args)
    return out.reshape(bsz, seq, d)


def kernel(x, p, ln_pre, ln_post, fox_w_in, fox_b_f, fox_w_out, s5_w_in, s5_lambda_re, s5_lambda_im,
           s5_log_dt, s5_b_re, s5_b_im, s5_c_re, s5_c_im, s5_d, s5_w_glu, s5_b_glu, s5_w_out,
           ple_w_proj, ple_w_gate, ple_b_gate):
    qa, ka, vt, z = _fox_proj(x, ln_pre[0], fox_w_in[0], fox_b_f[0])
    gated = _fox_attn(qa, ka, vt, z)
    x = _fox_post(gated, x, p, 0, fox_w_out[0], ln_post[0], ple_w_gate[0], ple_b_gate[0],
                  ple_w_proj[0])
    xg, z2d = _s5_front(x, ln_pre[1], s5_w_in[0])
    ops = _s5_operators(s5_lambda_re[0], s5_lambda_im[0], s5_log_dt[0], s5_b_re[0], s5_b_im[0],
                        s5_c_re[0], s5_c_im[0], s5_d[0])
    yg = _s5_ssm(xg, ops)
    return _s5_post(yg, z2d, x, p, 1, s5_w_glu[0], s5_b_glu[0], s5_w_out[0], ln_post[1],
                    ple_w_gate[1], ple_b_gate[1], ple_w_proj[1])
```

```python
import functools
import math

import jax
import jax.numpy as jnp
import numpy as np
from jax import lax
from jax.experimental import pallas as pl
from jax.experimental.pallas import tpu as pltpu

F32 = jnp.float32
BF16 = jnp.bfloat16

D_MODEL = 1024
PLE_DIM = 256
RMS_EPS = 1e-6
NEG_INF = -1e30

FOX_HEADS = 16
FOX_HEAD_DIM = 64
FOX_WIDTH = FOX_HEADS * FOX_HEAD_DIM
FOX_AUG_DIM = 128
FOX_AUG_WIDTH = FOX_HEADS * FOX_AUG_DIM

S5_GROUP = 16
S5_GROUPS = D_MODEL // S5_GROUP
S5_STATE = 64
S5_CHUNK = 16
S5_ROW = S5_CHUNK * S5_GROUP

V7X_SCOPED_VMEM_LIMIT_BYTES = 56 * 1024 * 1024

PROJ_ROWS = 256
POST_ROWS = 512
ATTN_Q_ROWS = 256
ATTN_KV_ROWS = PROJ_ROWS
ATTN_SCORE_SLOTS = 3
LOG2_E = 1.4426950408889634


def _const_spec(shape):
    zeros = (0,) * len(shape)
    return pl.BlockSpec(shape, lambda *_: zeros, pipeline_mode=pl.Buffered(1))


def _params(*semantics):
    return pltpu.CompilerParams(dimension_semantics=semantics,
                                vmem_limit_bytes=V7X_SCOPED_VMEM_LIMIT_BYTES)


def _dot(a, b):
    return jnp.dot(a, b, preferred_element_type=F32)


def _rmsnorm(x, g):
    var = jnp.mean(x * x, axis=-1, keepdims=True)
    return x * lax.rsqrt(var + RMS_EPS) * g


def _sigmoid(x):
    return 1.0 / (1.0 + jnp.exp(-x))


def _silu(x):
    return x * _sigmoid(x)


def _gelu_tanh(x):
    c = math.sqrt(2.0 / math.pi)
    return 0.5 * x * (1.0 + jnp.tanh(c * (x + 0.044715 * (x * x * x))))


def _split3(x):
    hi = x.astype(BF16).astype(F32)
    r1 = x - hi
    mid = r1.astype(BF16).astype(F32)
    lo = (r1 - mid).astype(BF16).astype(F32)
    return hi, mid, lo


def _fox_proj_kernel(x_ref, g_ref, wq_ref, wk_ref, wv_ref, wz_ref, wf_ref, bf_ref, eq_ref, ek_ref,
                     q_out, k_out, v_out, z_out, carry_ref):
    rows = x_ref.shape[1]

    @pl.when(pl.program_id(1) == 0)
    def _():
        carry_ref[...] = jnp.zeros_like(carry_ref)

    h = _rmsnorm(x_ref[0], g_ref[...]).astype(BF16)

    zf = _dot(h, wf_ref[...]) + bf_ref[...]
    log_f = jnp.minimum(zf, 0.0) - jnp.log(1.0 + jnp.exp(-jnp.abs(zf)))
    r = lax.broadcasted_iota(jnp.int32, (rows, rows), 0)
    c = lax.broadcasted_iota(jnp.int32, (rows, rows), 1)
    tri = jnp.where(r >= c, 1.0, 0.0).astype(BF16)
    hi, mid, lo = _split3(log_f)
    cum = (_dot(tri, hi.astype(BF16)) + _dot(tri, mid.astype(BF16))
           + _dot(tri, lo.astype(BF16))) + carry_ref[0:1, :]
    carry_ref[...] = jnp.broadcast_to(cum[rows - 1:rows, :], carry_ref.shape)

    c_hi, c_mid, c_lo = _split3(cum * LOG2_E)
    lane = lax.broadcasted_iota(jnp.int32, cum.shape, 1)
    aug = jnp.where(lane < 16, c_hi,
                    jnp.where(lane < 32, c_mid,
                              jnp.where(lane < 48, c_lo,
                                        jnp.where(lane == 48, 1.0, 0.0)))).astype(BF16)

    def widen(qk, placed):
        low = lax.broadcasted_iota(jnp.int32, (rows, FOX_AUG_DIM), 1) < FOX_HEAD_DIM
        tiles = []
        for j in range(FOX_HEADS // 2):
            pair = qk[:, j * FOX_AUG_DIM:(j + 1) * FOX_AUG_DIM]
            for a, val in enumerate((pair, pltpu.roll(pair, FOX_HEAD_DIM, 1))):
                lanes = slice((2 * j + a) * FOX_AUG_DIM, (2 * j + a + 1) * FOX_AUG_DIM)
                tiles.append(jnp.where(low, val, placed[:, lanes]))
        return jnp.concatenate(tiles, axis=1).astype(BF16)

    q_out[0] = widen(_dot(h, wq_ref[...]) * LOG2_E, _dot(aug, eq_ref[...]))
    k_out[0] = widen(_dot(h, wk_ref[...]), _dot(aug, ek_ref[...]))
    v_out[0, 0] = lax.dot_general(wv_ref[...], h, (((1,), (1,)), ((), ())),
                                  preferred_element_type=F32).astype(BF16)
    z_out[0] = _dot(h, wz_ref[...]).astype(BF16)


def _fox_aug_placement():
    eq = np.zeros((128, FOX_AUG_WIDTH), np.float32)
    ek = np.zeros((128, FOX_AUG_WIDTH), np.float32)
    for h in range(FOX_HEADS):
        base = h * FOX_AUG_DIM + FOX_HEAD_DIM
        for part in range(3):
            eq[part * 16 + h, base + part] = 1.0
            ek[part * 16 + h, base + 3 + part] = -1.0
            eq[48, base + 3 + part] = 1.0
            ek[48, base + part] = 1.0
    return jnp.asarray(eq, BF16), jnp.asarray(ek, BF16)


def _fox_proj(x, g, w_in, b_f):
    bsz, seq, d = x.shape
    scale = FOX_HEAD_DIM ** -0.5
    wq = (w_in[:, :FOX_WIDTH] * scale).astype(BF16)
    wk = w_in[:, FOX_WIDTH:2 * FOX_WIDTH].astype(BF16)
    wv = w_in[:, 2 * FOX_WIDTH:3 * FOX_WIDTH].T.astype(BF16)
    wz = w_in[:, 3 * FOX_WIDTH:4 * FOX_WIDTH].astype(BF16)
    wf = w_in[:, 4 * FOX_WIDTH:]
    wf = jnp.pad(jnp.tile(wf, (1, 3)), ((0, 0), (0, 128 - 3 * FOX_HEADS))).astype(BF16)
    bf = jnp.pad(jnp.tile(b_f, 3), (0, 128 - 3 * FOX_HEADS)).reshape(1, 128).astype(F32)
    eq, ek = _fox_aug_placement()

    t = PROJ_ROWS
    row_spec = lambda w: pl.BlockSpec((1, t, w), lambda b, i: (b, i, 0))
    return pl.pallas_call(
        _fox_proj_kernel,
        grid=(bsz, seq // t),
        in_specs=[row_spec(d), _const_spec((1, d)),
                  _const_spec(wq.shape), _const_spec(wk.shape), _const_spec(wv.shape),
                  _const_spec(wz.shape), _const_spec(wf.shape), _const_spec(bf.shape),
                  _const_spec(eq.shape), _const_spec(ek.shape)],
        out_specs=[row_spec(FOX_AUG_WIDTH), row_spec(FOX_AUG_WIDTH),
                   pl.BlockSpec((1, 1, FOX_WIDTH, t), lambda b, i: (b, i, 0, 0)),
                   row_spec(FOX_WIDTH)],
        out_shape=[jax.ShapeDtypeStruct((bsz, seq, FOX_AUG_WIDTH), BF16),
                   jax.ShapeDtypeStruct((bsz, seq, FOX_AUG_WIDTH), BF16),
                   jax.ShapeDtypeStruct((bsz, seq // t, FOX_WIDTH, t), BF16),
                   jax.ShapeDtypeStruct((bsz, seq, FOX_WIDTH), BF16)],
        scratch_shapes=[pltpu.VMEM((8, 128), F32)],
        compiler_params=_params("parallel", "arbitrary"),
        name="fox_proj",
    )(x, g.reshape(1, d), wq, wk, wv, wz, wf, bf, eq, ek)


def _fox_attn_kernel(q_ref, k_ref, vt_ref, z_ref, o_ref, st_ref, *, tq, tk):
    seq = q_ref.shape[1]
    heads = (0, 1)
    slots = st_ref.shape[0]
    steps = [(i, kb) for i in range(seq // tq) for kb in range(i + 1)]

    def issue_scores(t):
        i, kb = steps[t]
        for a in heads:
            lanes = slice(a * FOX_AUG_DIM, (a + 1) * FOX_AUG_DIM)
            k = k_ref[0, kb * tk:(kb + 1) * tk, lanes]
            q = q_ref[0, i * tq:(i + 1) * tq, lanes]
            st_ref[t % slots, a] = lax.dot_general(k, q, (((1,), (1,)), ((), ())),
                                                   preferred_element_type=F32)

    sum_rows = 16
    rid = lax.broadcasted_iota(jnp.int32, (sum_rows, tk), 0)
    ones_rows = jnp.where(rid == 0, 1.0, 0.0).astype(BF16)

    def update(a, kb, st, stats, masked):
        m, acc = stats
        if masked:
            kv = lax.broadcasted_iota(jnp.int32, st.shape, 0)
            qq = lax.broadcasted_iota(jnp.int32, st.shape, 1)
            st = jnp.where(kv <= qq, st, NEG_INF)
        m_new = jnp.maximum(m, jnp.max(st, axis=0, keepdims=True))
        alpha = jnp.exp2(m - m_new)
        p = jnp.exp2(st - m_new)
        vt = vt_ref[0, kb, a * FOX_HEAD_DIM:(a + 1) * FOX_HEAD_DIM, :]
        vt = jnp.concatenate([vt, ones_rows], axis=0)
        return m_new, alpha * acc + _dot(vt, p.astype(BF16))

    def init():
        return [(jnp.full((1, tq), NEG_INF, F32), jnp.zeros((FOX_HEAD_DIM + sum_rows, tq), F32))
                for _ in heads]

    for t in range(slots - 1):
        issue_scores(t)
    stats = init()
    for t, (i, kb) in enumerate(steps):
        if t + slots - 1 < len(steps):
            issue_scores(t + slots - 1)
        stats = [update(a, kb, st_ref[t % slots, a], stats[a], masked=(kb == i)) for a in heads]
        if kb == i:
            outs = [acc[:FOX_HEAD_DIM] / acc[FOX_HEAD_DIM:FOX_HEAD_DIM + 1] for _, acc in stats]
            o = jnp.concatenate(outs, axis=0).T
            rows = slice(i * tq, (i + 1) * tq)
            o_ref[0, rows, :] = (o * _silu(z_ref[0, rows, :].astype(F32))).astype(BF16)
            stats = init()


def _fox_attn(qa, ka, vt, z):
    bsz, seq, _ = z.shape
    tq = ATTN_Q_ROWS
    tk = ATTN_KV_ROWS
    assert tq == tk and vt.shape == (bsz, seq // tk, FOX_WIDTH, tk)
    pair = 2 * FOX_HEAD_DIM
    seq_spec = lambda w: pl.BlockSpec((1, seq, w), lambda b, j: (b, 0, j))
    return pl.pallas_call(
        functools.partial(_fox_attn_kernel, tq=tq, tk=tk),
        grid=(bsz, FOX_HEADS // 2),
        in_specs=[seq_spec(2 * FOX_AUG_DIM), seq_spec(2 * FOX_AUG_DIM),
                  pl.BlockSpec((1, seq // tk, pair, tk), lambda b, j: (b, 0, j, 0)),
                  seq_spec(pair)],
        out_specs=seq_spec(pair),
        out_shape=jax.ShapeDtypeStruct((bsz, seq, FOX_WIDTH), BF16),
        scratch_shapes=[pltpu.VMEM((ATTN_SCORE_SLOTS, 2, tk, tq), F32)],
        compiler_params=_params("parallel", "parallel"),
        name="fox_attn",
    )(qa, ka, vt, z)


def _slot_masks(batch):
    slot = lax.broadcasted_iota(jnp.int32, (batch, 128), 1) >> 4
    return [slot == q for q in range(128 // S5_GROUP)]


def _s5_front_kernel(x_ref, g_ref, perm_ref, wu_ref, wz_ref, xg_out, z_out):
    batch, _, d = x_ref.shape
    slots = 128 // S5_GROUP
    h = _rmsnorm(x_ref[...].reshape(batch * S5_CHUNK, d), g_ref[...]).astype(BF16)
    h = _dot(perm_ref[...], h).astype(BF16)
    z_out[0] = _dot(h, wz_ref[...]).astype(BF16)
    u = _dot(h, wu_ref[...])
    masks = _slot_masks(batch)
    rolled = []
    for s in range(S5_CHUNK):
        shift = (s % slots) * S5_GROUP
        tiles = [u[s * batch:(s + 1) * batch, j * 128:(j + 1) * 128] for j in range(d // 128)]
        rolled.append([pltpu.roll(t, shift, 1) if shift else t for t in tiles])
    for g in range(S5_GROUPS):
        j, r = divmod(g, slots)
        halves = []
        for k in range(S5_CHUNK // slots):
            acc = rolled[slots * k + (-r) % slots][j]
            for q in range(1, slots):
                acc = jnp.where(masks[q], rolled[slots * k + (q - r) % slots][j], acc)
            halves.append(acc)
        xg_out[g, 0] = jnp.concatenate(halves, axis=1).astype(BF16)


def _token_major_perm(bsz):
    n = np.arange(bsz * S5_CHUNK)
    perm = np.zeros((n.size, n.size), np.float32)
    perm[n, (n % bsz) * S5_CHUNK + n // bsz] = 1.0
    return perm


def _s5_front(x, g, w_in):
    bsz, seq, d = x.shape
    chunks = seq // S5_CHUNK
    rows = bsz * S5_CHUNK
    wu = w_in[:, :d].astype(BF16)
    wz = w_in[:, d:].astype(BF16)
    perm = jnp.asarray(_token_major_perm(bsz), BF16)
    return pl.pallas_call(
        _s5_front_kernel,
        grid=(chunks,),
        in_specs=[pl.BlockSpec((bsz, S5_CHUNK, d), lambda c: (0, c, 0)), _const_spec((1, d)),
                  _const_spec(perm.shape), _const_spec(wu.shape), _const_spec(wz.shape)],
        out_specs=[pl.BlockSpec((S5_GROUPS, 1, bsz, S5_ROW), lambda c: (0, c, 0, 0)),
                   pl.BlockSpec((1, rows, d), lambda c: (c, 0, 0))],
        out_shape=[jax.ShapeDtypeStruct((S5_GROUPS, chunks, bsz, S5_ROW), BF16),
                   jax.ShapeDtypeStruct((chunks, rows, d), BF16)],
        compiler_params=_params("parallel"),
        name="s5_front",
    )(x, g.reshape(1, d), perm, wu, wz)


def _s5_group_kernel(x_ref, w_ref, bst_ref, cst_ref, ar_ref, ai_ref, y_ref,
                     sre_ref, sim_ref, pre_ref, pim_ref, *, batch, chunks):
    x = x_ref[0]
    s = _dot(x, bst_ref[0])
    sre_ref[...] = s[:, :S5_STATE]
    sim_ref[...] = s[:, S5_STATE:]
    ar = ar_ref[0]
    ai = ai_ref[0]

    def body(c, carry):
        xr, xi = carry
        rows = pl.ds(pl.multiple_of(c * batch, batch), batch)
        pre_ref[rows, :] = xr
        pim_ref[rows, :] = xi
        nxr = ar * xr - ai * xi + sre_ref[rows, :]
        nxi = ar * xi + ai * xr + sim_ref[rows, :]
        return nxr, nxi

    zero = jnp.zeros((batch, S5_STATE), F32)
    lax.fori_loop(0, chunks, body, (zero, zero))
    prev = jnp.concatenate([pre_ref[...], pim_ref[...]], axis=1).astype(BF16)
    y_ref[0] = _dot(x, w_ref[0]) + _dot(prev, cst_ref[0])


def _s5_operators(lam_re, lam_im, log_dt, b_re, b_im, c_re, c_im, d_skip):
    hp = lax.Precision.HIGHEST
    t = S5_CHUNK
    dt = jnp.exp(log_dt)[:, None]
    tau = jnp.arange(t + 1, dtype=F32)[:, None, None]
    mag = jnp.exp(lam_re * dt * tau)
    ang = lam_im * dt * tau
    pw_re = mag * jnp.cos(ang)
    pw_im = mag * jnp.sin(ang)
    lb_re, lb_im = pw_re[1], pw_im[1]
    den = lam_re * lam_re + lam_im * lam_im
    num_re = lb_re - 1.0
    num_im = lb_im
    coef_re = (num_re * lam_re + num_im * lam_im) / den
    coef_im = (num_im * lam_re - num_re * lam_im) / den
    bb_re = coef_re[..., None] * b_re - coef_im[..., None] * b_im
    bb_im = coef_re[..., None] * b_im + coef_im[..., None] * b_re
    m_re = pw_re[..., None] * bb_re - pw_im[..., None] * bb_im
    m_im = pw_re[..., None] * bb_im + pw_im[..., None] * bb_re
    kern = (jnp.einsum('gop,tgpi->tgoi', c_re, m_re[:t], precision=hp)
            - jnp.einsum('gop,tgpi->tgoi', c_im, m_im[:t], precision=hp))
    skip = d_skip.reshape(S5_GROUPS, S5_GROUP)[:, :, None] * jnp.eye(S5_GROUP, dtype=F32)
    kern = kern.at[0].add(skip)
    zeros = jnp.zeros_like(kern)
    toep = jnp.stack([jnp.concatenate([zeros[:s], kern[:t - s]], axis=0) for s in range(t)])
    w = toep.transpose(2, 0, 4, 1, 3).reshape(S5_GROUPS, S5_ROW, S5_ROW)
    bst_re = m_re[:t][::-1].transpose(1, 0, 3, 2).reshape(S5_GROUPS, S5_ROW, S5_STATE)
    bst_im = m_im[:t][::-1].transpose(1, 0, 3, 2).reshape(S5_GROUPS, S5_ROW, S5_STATE)
    bst = jnp.concatenate([bst_re, bst_im], axis=2)
    pr = pw_re[1:].transpose(1, 2, 0)[:, :, :, None]
    pi = pw_im[1:].transpose(1, 2, 0)[:, :, :, None]
    cr = c_re.transpose(0, 2, 1)[:, :, None, :]
    ci = c_im.transpose(0, 2, 1)[:, :, None, :]
    cst_re = (cr * pr - ci * pi).reshape(S5_GROUPS, S5_STATE, S5_ROW)
    cst_im = (-(cr * pi + ci * pr)).reshape(S5_GROUPS, S5_STATE, S5_ROW)
    cst = jnp.concatenate([cst_re, cst_im], axis=1)
    def front_order(a):
        cols = a.shape[-1]
        a = a.reshape(S5_GROUPS // 8, 8, t // 8, 8, S5_GROUP * cols)
        a = jnp.stack([jnp.roll(a[:, r], r, axis=2) for r in range(8)], axis=1)
        return a.reshape(S5_GROUPS, S5_ROW, cols)

    w = front_order(w)
    bst = front_order(bst)
    a_re = pw_re[t].reshape(S5_GROUPS, 1, S5_STATE)
    a_im = pw_im[t].reshape(S5_GROUPS, 1, S5_STATE)
    return w.astype(BF16), bst.astype(BF16), cst.astype(BF16), a_re, a_im


def _s5_ssm(xg, ops):
    _, chunks, bsz, _ = xg.shape
    rows = chunks * bsz
    g_spec = lambda a, b: pl.BlockSpec((1, a, b), lambda g: (g, 0, 0))
    yg = pl.pallas_call(
        functools.partial(_s5_group_kernel, batch=bsz, chunks=chunks),
        grid=(S5_GROUPS,),
        in_specs=[g_spec(rows, S5_ROW), g_spec(S5_ROW, S5_ROW), g_spec(S5_ROW, 2 * S5_STATE),
                  g_spec(2 * S5_STATE, S5_ROW), g_spec(1, S5_STATE), g_spec(1, S5_STATE)],
        out_specs=g_spec(rows, S5_ROW),
        out_shape=jax.ShapeDtypeStruct((S5_GROUPS, rows, S5_ROW), F32),
        scratch_shapes=[pltpu.VMEM((rows, S5_STATE), F32)] * 4,
        compiler_params=_params("parallel"),
        name="s5_ssm",
    )(xg.reshape(S5_GROUPS, rows, S5_ROW), *ops)
    return yg.reshape(S5_GROUPS, chunks, bsz, S5_ROW)


def _residual_ple(mix, x, p, wo_ref, g_ref, wg_ref, bg_ref, wp_ref):
    x1 = x + _rmsnorm(_dot(mix, wo_ref[...]), g_ref[...])
    gate = _sigmoid(_dot(x1.astype(BF16), wg_ref[...]) + bg_ref[...])
    return x1 + _dot(p.astype(BF16), wp_ref[...]) * gate


def _fox_post_kernel(a_ref, x_ref, p_ref, wo_ref, g_ref, wg_ref, bg_ref, wp_ref, o_ref):
    o_ref[0] = _residual_ple(a_ref[0], x_ref[0], p_ref[0, 0], wo_ref, g_ref, wg_ref, bg_ref, wp_ref)


def _s5_post_kernel(yg_ref, z_ref, x_ref, p_ref, perm_ref, wglu_ref, bglu_ref, wo_ref, g_ref,
                    wg_ref, bg_ref, wp_ref, o_ref):
    batch, _, d = x_ref.shape
    slots = 128 // S5_GROUP
    slot = lax.broadcasted_iota(jnp.int32, (batch, 128), 1) >> 4
    tok = [[None] * (d // 128) for _ in range(S5_CHUNK)]
    for j in range(d // 128):
        for k in range(S5_CHUNK // slots):
            tiles = [yg_ref[slots * j + r, 0, :, k * 128:(k + 1) * 128] for r in range(slots)]
            for dist in (4, 2, 1):
                low = (slot & (2 * dist - 1)) < dist
                nxt = list(tiles)
                for r in range(slots):
                    if not r & dist:
                        a, b = tiles[r], tiles[r + dist]
                        nxt[r] = jnp.where(low, a, pltpu.roll(b, dist * S5_GROUP, 1))
                        nxt[r + dist] = jnp.where(low, pltpu.roll(a, 128 - dist * S5_GROUP, 1), b)
                tiles = nxt
            for tq in range(slots):
                tok[slots * k + tq][j] = tiles[tq]
    y = jnp.concatenate([jnp.concatenate(t, axis=1) for t in tok], axis=0)
    gl = _gelu_tanh(y)
    gl = gl * _sigmoid(_dot(gl.astype(BF16), wglu_ref[...]) + bglu_ref[...])
    mix = (gl * _silu(z_ref[0].astype(F32))).astype(BF16)
    mix = _dot(perm_ref[...], mix).astype(BF16)
    rows = batch * S5_CHUNK
    out = _residual_ple(mix, x_ref[...].reshape(rows, d), p_ref[0].reshape(rows, PLE_DIM),
                        wo_ref, g_ref, wg_ref, bg_ref, wp_ref)
    o_ref[...] = out.reshape(batch, S5_CHUNK, d)


def _tail_weights(w_out, g_post, w_gate, b_gate, w_proj):
    d = w_out.shape[0]
    args = [w_out.astype(BF16), g_post.reshape(1, d), w_gate.astype(BF16), b_gate.reshape(1, d),
            w_proj.astype(BF16)]
    return args, [_const_spec(a.shape) for a in args]


def _fox_post(gated, x, p, layer, *tail):
    bsz, seq, d = x.shape
    t = POST_ROWS
    row_spec = pl.BlockSpec((1, t, d), lambda b, i: (b, i, 0))
    p_spec = pl.BlockSpec((1, 1, t, PLE_DIM), lambda b, i: (layer, b, i, 0))
    w_args, w_specs = _tail_weights(*tail)
    return pl.pallas_call(
        _fox_post_kernel,
        grid=(bsz, seq // t),
        in_specs=[row_spec, row_spec, p_spec] + w_specs,
        out_specs=row_spec,
        out_shape=jax.ShapeDtypeStruct((bsz, seq, d), F32),
        compiler_params=_params("parallel", "parallel"),
        name="fox_post",
    )(gated, x, p, *w_args)


def _s5_post(yg, z, x, p, layer, w_glu, b_glu, *tail):
    bsz, seq, d = x.shape
    chunks = seq // S5_CHUNK
    rows = bsz * S5_CHUNK
    chunk_spec = pl.BlockSpec((bsz, S5_CHUNK, d), lambda c: (0, c, 0))
    p_spec = pl.BlockSpec((1, bsz, S5_CHUNK, PLE_DIM), lambda c: (layer, 0, c, 0))
    perm = jnp.asarray(_token_major_perm(bsz).T, BF16)
    w_args, w_specs = _tail_weights(*tail)
    return pl.pallas_call(
        _s5_post_kernel,
        grid=(chunks,),
        in_specs=[pl.BlockSpec((S5_GROUPS, 1, bsz, S5_ROW), lambda c: (0, c, 0, 0)),
                  pl.BlockSpec((1, rows, d), lambda c: (c, 0, 0)), chunk_spec, p_spec,
                  _const_spec(perm.shape), _const_spec((d, d)), _const_spec((1, d))] + w_specs,
        out_specs=chunk_spec,
        out_shape=jax.ShapeDtypeStruct((bsz, seq, d), F32),
        compiler_params=_params("parallel"),
        name="s5_post",
    )(yg, z, x, p, perm, w_glu.astype(BF16), b_glu.reshape(1, d), *w_args)


def kernel(x, p, ln_pre, ln_post, fox_w_in, fox_b_f, fox_w_out, s5_w_in, s5_lambda_re, s5_lambda_im,
           s5_log_dt, s5_b_re, s5_b_im, s5_c_re, s5_c_im, s5_d, s5_w_glu, s5_b_glu, s5_w_out,
           ple_w_proj, ple_w_gate, ple_b_gate):
    qa, ka, vt, z = _fox_proj(x, ln_pre[0], fox_w_in[0], fox_b_f[0])
    gated = _fox_attn(qa, ka, vt, z)
    x = _fox_post(gated, x, p, 0, fox_w_out[0], ln_post[0], ple_w_gate[0], ple_b_gate[0],
                  ple_w_proj[0])
    xg, z = _s5_front(x, ln_pre[1], s5_w_in[0])
    ops = _s5_operators(s5_lambda_re[0], s5_lambda_im[0], s5_log_dt[0], s5_b_re[0], s5_b_im[0],
                        s5_c_re[0], s5_c_im[0], s5_d[0])
    yg = _s5_ssm(xg, ops)
    return _s5_post(yg, z, x, p, 1, s5_w_glu[0], s5_b_glu[0], s5_w_out[0], ln_post[1],
                    ple_w_gate[1], ple_b_gate[1], ple_w_proj[1])
```

```python
import functools
import math

import jax
import jax.numpy as jnp
import numpy as np
from jax import lax
from jax.experimental import pallas as pl
from jax.experimental.pallas import tpu as pltpu

F32 = jnp.float32
BF16 = jnp.bfloat16

D_MODEL = 1024
PLE_DIM = 256
RMS_EPS = 1e-6
NEG_INF = -1e30

FOX_HEADS = 16
FOX_HEAD_DIM = 64
FOX_WIDTH = FOX_HEADS * FOX_HEAD_DIM
FOX_AUG_DIM = 128
FOX_AUG_WIDTH = FOX_HEADS * FOX_AUG_DIM

S5_GROUP = 16
S5_GROUPS = D_MODEL // S5_GROUP
S5_STATE = 64
S5_CHUNK = 16
S5_ROW = S5_CHUNK * S5_GROUP
S5_SSM_GROUPS_PER_STEP = 2

V7X_SCOPED_VMEM_LIMIT_BYTES = 56 * 1024 * 1024

PROJ_ROWS = 256
POST_ROWS = 512
ATTN_Q_ROWS = 256
ATTN_KV_ROWS = PROJ_ROWS
ATTN_SCORE_SLOTS = 3
LOG2_E = 1.4426950408889634


def _const_spec(shape):
    zeros = (0,) * len(shape)
    return pl.BlockSpec(shape, lambda *_: zeros, pipeline_mode=pl.Buffered(1))


def _params(*semantics):
    return pltpu.CompilerParams(dimension_semantics=semantics,
                                vmem_limit_bytes=V7X_SCOPED_VMEM_LIMIT_BYTES)


def _dot(a, b):
    return jnp.dot(a, b, preferred_element_type=F32)


def _rmsnorm(x, g):
    var = jnp.mean(x * x, axis=-1, keepdims=True)
    return x * lax.rsqrt(var + RMS_EPS) * g


def _sigmoid(x):
    return 1.0 / (1.0 + jnp.exp(-x))


def _silu(x):
    return x * _sigmoid(x)


def _gelu_tanh(x):
    c = math.sqrt(2.0 / math.pi)
    return 0.5 * x * (1.0 + jnp.tanh(c * (x + 0.044715 * (x * x * x))))


def _split3(x):
    hi = x.astype(BF16).astype(F32)
    r1 = x - hi
    mid = r1.astype(BF16).astype(F32)
    lo = (r1 - mid).astype(BF16).astype(F32)
    return hi, mid, lo


def _fox_proj_kernel(x_ref, g_ref, wq_ref, wk_ref, wv_ref, wz_ref, wf_ref, bf_ref, eq_ref, ek_ref,
                     q_out, k_out, v_out, z_out, carry_ref):
    rows = x_ref.shape[1]

    @pl.when(pl.program_id(1) == 0)
    def _():
        carry_ref[...] = jnp.zeros_like(carry_ref)

    h = _rmsnorm(x_ref[0], g_ref[...]).astype(BF16)

    zf = _dot(h, wf_ref[...]) + bf_ref[...]
    log_f = jnp.minimum(zf, 0.0) - jnp.log(1.0 + jnp.exp(-jnp.abs(zf)))
    r = lax.broadcasted_iota(jnp.int32, (rows, rows), 0)
    c = lax.broadcasted_iota(jnp.int32, (rows, rows), 1)
    tri = jnp.where(r >= c, 1.0, 0.0).astype(BF16)
    hi, mid, lo = _split3(log_f)
    cum = (_dot(tri, hi.astype(BF16)) + _dot(tri, mid.astype(BF16))
           + _dot(tri, lo.astype(BF16))) + carry_ref[0:1, :]
    carry_ref[...] = jnp.broadcast_to(cum[rows - 1:rows, :], carry_ref.shape)

    c_hi, c_mid, c_lo = _split3(cum * LOG2_E)
    lane = lax.broadcasted_iota(jnp.int32, cum.shape, 1)
    aug = jnp.where(lane < 16, c_hi,
                    jnp.where(lane < 32, c_mid,
                              jnp.where(lane < 48, c_lo,
                                        jnp.where(lane == 48, 1.0, 0.0)))).astype(BF16)

    def widen(qk, placed):
        low = lax.broadcasted_iota(jnp.int32, (rows, FOX_AUG_DIM), 1) < FOX_HEAD_DIM
        tiles = []
        for j in range(FOX_HEADS // 2):
            pair = qk[:, j * FOX_AUG_DIM:(j + 1) * FOX_AUG_DIM]
            for a, val in enumerate((pair, pltpu.roll(pair, FOX_HEAD_DIM, 1))):
                lanes = slice((2 * j + a) * FOX_AUG_DIM, (2 * j + a + 1) * FOX_AUG_DIM)
                tiles.append(jnp.where(low, val, placed[:, lanes]))
        return jnp.concatenate(tiles, axis=1).astype(BF16)

    q_out[0] = widen(_dot(h, wq_ref[...]) * LOG2_E, _dot(aug, eq_ref[...]))
    k_out[0] = widen(_dot(h, wk_ref[...]), _dot(aug, ek_ref[...]))
    v_out[0, 0] = lax.dot_general(wv_ref[...], h, (((1,), (1,)), ((), ())),
                                  preferred_element_type=F32).astype(BF16)
    z_out[0] = _dot(h, wz_ref[...]).astype(BF16)


def _fox_aug_placement():
    eq = np.zeros((128, FOX_AUG_WIDTH), np.float32)
    ek = np.zeros((128, FOX_AUG_WIDTH), np.float32)
    for h in range(FOX_HEADS):
        base = h * FOX_AUG_DIM + FOX_HEAD_DIM
        for part in range(3):
            eq[part * 16 + h, base + part] = 1.0
            ek[part * 16 + h, base + 3 + part] = -1.0
            eq[48, base + 3 + part] = 1.0
            ek[48, base + part] = 1.0
    return jnp.asarray(eq, BF16), jnp.asarray(ek, BF16)


def _fox_proj(x, g, w_in, b_f):
    bsz, seq, d = x.shape
    scale = FOX_HEAD_DIM ** -0.5
    wq = (w_in[:, :FOX_WIDTH] * scale).astype(BF16)
    wk = w_in[:, FOX_WIDTH:2 * FOX_WIDTH].astype(BF16)
    wv = w_in[:, 2 * FOX_WIDTH:3 * FOX_WIDTH].T.astype(BF16)
    wz = w_in[:, 3 * FOX_WIDTH:4 * FOX_WIDTH].astype(BF16)
    wf = w_in[:, 4 * FOX_WIDTH:]
    wf = jnp.pad(jnp.tile(wf, (1, 3)), ((0, 0), (0, 128 - 3 * FOX_HEADS))).astype(BF16)
    bf = jnp.pad(jnp.tile(b_f, 3), (0, 128 - 3 * FOX_HEADS)).reshape(1, 128).astype(F32)
    eq, ek = _fox_aug_placement()

    t = PROJ_ROWS
    row_spec = lambda w: pl.BlockSpec((1, t, w), lambda b, i: (b, i, 0))
    return pl.pallas_call(
        _fox_proj_kernel,
        grid=(bsz, seq // t),
        in_specs=[row_spec(d), _const_spec((1, d)),
                  _const_spec(wq.shape), _const_spec(wk.shape), _const_spec(wv.shape),
                  _const_spec(wz.shape), _const_spec(wf.shape), _const_spec(bf.shape),
                  _const_spec(eq.shape), _const_spec(ek.shape)],
        out_specs=[row_spec(FOX_AUG_WIDTH), row_spec(FOX_AUG_WIDTH),
                   pl.BlockSpec((1, 1, FOX_WIDTH, t), lambda b, i: (b, i, 0, 0)),
                   row_spec(FOX_WIDTH)],
        out_shape=[jax.ShapeDtypeStruct((bsz, seq, FOX_AUG_WIDTH), BF16),
                   jax.ShapeDtypeStruct((bsz, seq, FOX_AUG_WIDTH), BF16),
                   jax.ShapeDtypeStruct((bsz, seq // t, FOX_WIDTH, t), BF16),
                   jax.ShapeDtypeStruct((bsz, seq, FOX_WIDTH), BF16)],
        scratch_shapes=[pltpu.VMEM((8, 128), F32)],
        compiler_params=_params("parallel", "arbitrary"),
        name="fox_proj",
    )(x, g.reshape(1, d), wq, wk, wv, wz, wf, bf, eq, ek)


def _fox_attn_kernel(q_ref, k_ref, vt_ref, z_ref, o_ref, st_ref, *, tq, tk):
    seq = q_ref.shape[1]
    heads = (0, 1)
    slots = st_ref.shape[0]
    steps = [(i, kb) for i in range(seq // tq) for kb in range(i + 1)]

    def issue_scores(t):
        i, kb = steps[t]
        for a in heads:
            lanes = slice(a * FOX_AUG_DIM, (a + 1) * FOX_AUG_DIM)
            k = k_ref[0, kb * tk:(kb + 1) * tk, lanes]
            q = q_ref[0, i * tq:(i + 1) * tq, lanes]
            st_ref[t % slots, a] = lax.dot_general(k, q, (((1,), (1,)), ((), ())),
                                                   preferred_element_type=F32)

    sum_rows = 16
    rid = lax.broadcasted_iota(jnp.int32, (sum_rows, tk), 0)
    ones_rows = jnp.where(rid == 0, 1.0, 0.0).astype(BF16)

    def update(a, kb, st, stats, masked):
        m, acc = stats
        if masked:
            kv = lax.broadcasted_iota(jnp.int32, st.shape, 0)
            qq = lax.broadcasted_iota(jnp.int32, st.shape, 1)
            st = jnp.where(kv <= qq, st, NEG_INF)
        m_new = jnp.maximum(m, jnp.max(st, axis=0, keepdims=True))
        alpha = jnp.exp2(m - m_new)
        p = jnp.exp2(st - m_new)
        vt = vt_ref[0, kb, a * FOX_HEAD_DIM:(a + 1) * FOX_HEAD_DIM, :]
        vt = jnp.concatenate([vt, ones_rows], axis=0)
        return m_new, alpha * acc + _dot(vt, p.astype(BF16))

    def init():
        return [(jnp.full((1, tq), NEG_INF, F32), jnp.zeros((FOX_HEAD_DIM + sum_rows, tq), F32))
                for _ in heads]

    for t in range(slots - 1):
        issue_scores(t)
    stats = init()
    for t, (i, kb) in enumerate(steps):
        if t + slots - 1 < len(steps):
            issue_scores(t + slots - 1)
        stats = [update(a, kb, st_ref[t % slots, a], stats[a], masked=(kb == i)) for a in heads]
        if kb == i:
            outs = [acc[:FOX_HEAD_DIM] / acc[FOX_HEAD_DIM:FOX_HEAD_DIM + 1] for _, acc in stats]
            o = jnp.concatenate(outs, axis=0).T
            rows = slice(i * tq, (i + 1) * tq)
            o_ref[0, rows, :] = (o * _silu(z_ref[0, rows, :].astype(F32))).astype(BF16)
            stats = init()


def _fox_attn(qa, ka, vt, z):
    bsz, seq, _ = z.shape
    tq = ATTN_Q_ROWS
    tk = ATTN_KV_ROWS
    assert tq == tk and vt.shape == (bsz, seq // tk, FOX_WIDTH, tk)
    pair = 2 * FOX_HEAD_DIM
    seq_spec = lambda w: pl.BlockSpec((1, seq, w), lambda b, j: (b, 0, j))
    return pl.pallas_call(
        functools.partial(_fox_attn_kernel, tq=tq, tk=tk),
        grid=(bsz, FOX_HEADS // 2),
        in_specs=[seq_spec(2 * FOX_AUG_DIM), seq_spec(2 * FOX_AUG_DIM),
                  pl.BlockSpec((1, seq // tk, pair, tk), lambda b, j: (b, 0, j, 0)),
                  seq_spec(pair)],
        out_specs=seq_spec(pair),
        out_shape=jax.ShapeDtypeStruct((bsz, seq, FOX_WIDTH), BF16),
        scratch_shapes=[pltpu.VMEM((ATTN_SCORE_SLOTS, 2, tk, tq), F32)],
        compiler_params=_params("parallel", "parallel"),
        name="fox_attn",
    )(qa, ka, vt, z)


def _slot_masks(batch):
    slot = lax.broadcasted_iota(jnp.int32, (batch, 128), 1) >> 4
    return [slot == q for q in range(128 // S5_GROUP)]


def _s5_front_kernel(x_ref, g_ref, perm_ref, wu_ref, wz_ref, xg_out, z_out):
    batch, _, d = x_ref.shape
    slots = 128 // S5_GROUP
    h = _rmsnorm(x_ref[...].reshape(batch * S5_CHUNK, d), g_ref[...]).astype(BF16)
    h = _dot(perm_ref[...], h).astype(BF16)
    z_out[0] = _dot(h, wz_ref[...]).astype(BF16)
    u = _dot(h, wu_ref[...])
    masks = _slot_masks(batch)
    rolled = []
    for s in range(S5_CHUNK):
        shift = (s % slots) * S5_GROUP
        tiles = [u[s * batch:(s + 1) * batch, j * 128:(j + 1) * 128] for j in range(d // 128)]
        rolled.append([pltpu.roll(t, shift, 1) if shift else t for t in tiles])
    for g in range(S5_GROUPS):
        j, r = divmod(g, slots)
        halves = []
        for k in range(S5_CHUNK // slots):
            acc = rolled[slots * k + (-r) % slots][j]
            for q in range(1, slots):
                acc = jnp.where(masks[q], rolled[slots * k + (q - r) % slots][j], acc)
            halves.append(acc)
        xg_out[g, 0] = jnp.concatenate(halves, axis=1).astype(BF16)


def _token_major_perm(bsz):
    n = np.arange(bsz * S5_CHUNK)
    perm = np.zeros((n.size, n.size), np.float32)
    perm[n, (n % bsz) * S5_CHUNK + n // bsz] = 1.0
    return perm


def _s5_front(x, g, w_in):
    bsz, seq, d = x.shape
    chunks = seq // S5_CHUNK
    rows = bsz * S5_CHUNK
    wu = w_in[:, :d].astype(BF16)
    wz = w_in[:, d:].astype(BF16)
    perm = jnp.asarray(_token_major_perm(bsz), BF16)
    return pl.pallas_call(
        _s5_front_kernel,
        grid=(chunks,),
        in_specs=[pl.BlockSpec((bsz, S5_CHUNK, d), lambda c: (0, c, 0)), _const_spec((1, d)),
                  _const_spec(perm.shape), _const_spec(wu.shape), _const_spec(wz.shape)],
        out_specs=[pl.BlockSpec((S5_GROUPS, 1, bsz, S5_ROW), lambda c: (0, c, 0, 0)),
                   pl.BlockSpec((1, rows, d), lambda c: (c, 0, 0))],
        out_shape=[jax.ShapeDtypeStruct((S5_GROUPS, chunks, bsz, S5_ROW), BF16),
                   jax.ShapeDtypeStruct((chunks, rows, d), BF16)],
        compiler_params=_params("parallel"),
        name="s5_front",
    )(x, g.reshape(1, d), perm, wu, wz)


def _s5_group_kernel(x_ref, w_ref, bst_ref, cst_ref, ar_ref, ai_ref, y_ref,
                     sre_ref, sim_ref, pre_ref, pim_ref, *, batch, chunks):
    groups = range(x_ref.shape[0])
    xs = [x_ref[g] for g in groups]
    for g in groups:
        s = _dot(xs[g], bst_ref[g])
        sre_ref[g] = s[:, :S5_STATE]
        sim_ref[g] = s[:, S5_STATE:]
    ar = [ar_ref[g] for g in groups]
    ai = [ai_ref[g] for g in groups]
    zero = jnp.zeros((batch, S5_STATE), F32)
    state = [(zero, zero) for _ in groups]
    for c in range(chunks):
        rows = slice(c * batch, (c + 1) * batch)
        for g in groups:
            xr, xi = state[g]
            pre_ref[g, rows, :] = xr
            pim_ref[g, rows, :] = xi
            state[g] = (ar[g] * xr - ai[g] * xi + sre_ref[g, rows, :],
                        ar[g] * xi + ai[g] * xr + sim_ref[g, rows, :])
    ys = [_dot(xs[g], w_ref[g]) for g in groups]
    for g in groups:
        prev = jnp.concatenate([pre_ref[g], pim_ref[g]], axis=1).astype(BF16)
        y_ref[g] = ys[g] + _dot(prev, cst_ref[g])


def _s5_operator_kernel(lre_ref, lim_ref, ldt_ref, brt_ref, bit_ref, cre_ref, cim_ref, d_ref,
                        w_ref, bst_ref, cst_ref, ar_ref, ai_ref):
    t = S5_CHUNK
    hp = lax.Precision.HIGHEST
    nt = (((1,), (1,)), ((), ()))
    tau = lax.broadcasted_iota(jnp.int32, (24, S5_STATE), 0).astype(F32)
    row = lax.broadcasted_iota(jnp.int32, (S5_GROUP, S5_ROW), 0)
    lane = lax.broadcasted_iota(jnp.int32, (S5_GROUP, S5_ROW), 1)
    for r in range(w_ref.shape[0]):
        lr, li = lre_ref[r], lim_ref[r]
        dt = jnp.exp(ldt_ref[r])
        mag = jnp.exp(lr * dt * tau)
        ang = li * dt * tau
        pw_re = mag * jnp.cos(ang)
        pw_im = mag * jnp.sin(ang)
        pr = [pw_re[n:n + 1] for n in range(t + 1)]
        pi = [pw_im[n:n + 1] for n in range(t + 1)]
        den = lr * lr + li * li
        num_re = pr[1] - 1.0
        num_im = pi[1]
        coef_re = (num_re * lr + num_im * li) / den
        coef_im = (num_im * lr - num_re * li) / den
        brt, bit = brt_ref[r], bit_ref[r]
        bb_re = coef_re * brt - coef_im * bit
        bb_im = coef_re * bit + coef_im * brt
        cr, ci = cre_ref[r], cim_ref[r]
        c_pw_re = [cr * pr[n] - ci * pi[n] for n in range(t + 1)]
        c_pw_im = [-(cr * pi[n] + ci * pr[n]) for n in range(t + 1)]
        kt = (lax.dot_general(bb_re, jnp.concatenate(c_pw_re[:t], axis=0), nt, precision=hp,
                              preferred_element_type=F32)
              + lax.dot_general(bb_im, jnp.concatenate(c_pw_im[:t], axis=0), nt, precision=hp,
                                preferred_element_type=F32))
        kt = kt + jnp.where(lane == row, d_ref[r], 0.0)
        for s in range(t):
            off = (8 * (s // 8) + (s % 8 + r) % 8) * S5_GROUP
            blk = kt if s == 0 else jnp.where(lane >= s * S5_GROUP,
                                              pltpu.roll(kt, s * S5_GROUP, 1), 0.0)
            w_ref[r, off:off + S5_GROUP, :] = blk.astype(BF16)
            n = t - 1 - s
            m_re = bb_re * pr[n] - bb_im * pi[n]
            m_im = bb_re * pi[n] + bb_im * pr[n]
            bst_ref[r, off:off + S5_GROUP, :] = jnp.concatenate([m_re, m_im], axis=1).astype(BF16)
        readout = jnp.concatenate([jnp.concatenate(c_pw_re[1:], axis=0),
                                   jnp.concatenate(c_pw_im[1:], axis=0)], axis=1)
        cst_ref[r] = readout.T.astype(BF16)
        ar_ref[r] = pr[t]
        ai_ref[r] = pi[t]


def _s5_operators(lam_re, lam_im, log_dt, b_re, b_im, c_re, c_im, d_skip):
    g = S5_GROUPS
    n = 128 // S5_GROUP
    spec = lambda a, b: pl.BlockSpec((n, a, b), lambda j: (j, 0, 0))
    d_row = jnp.pad(d_skip.reshape(g, 1, S5_GROUP), ((0, 0), (0, 0), (0, S5_ROW - S5_GROUP)))
    return pl.pallas_call(
        _s5_operator_kernel,
        grid=(g // n,),
        in_specs=[spec(1, S5_STATE), spec(1, S5_STATE), spec(1, 1), spec(S5_GROUP, S5_STATE),
                  spec(S5_GROUP, S5_STATE), spec(S5_GROUP, S5_STATE), spec(S5_GROUP, S5_STATE),
                  spec(1, S5_ROW)],
        out_specs=[spec(S5_ROW, S5_ROW), spec(S5_ROW, 2 * S5_STATE), spec(2 * S5_STATE, S5_ROW),
                   spec(1, S5_STATE), spec(1, S5_STATE)],
        out_shape=[jax.ShapeDtypeStruct((g, S5_ROW, S5_ROW), BF16),
                   jax.ShapeDtypeStruct((g, S5_ROW, 2 * S5_STATE), BF16),
                   jax.ShapeDtypeStruct((g, 2 * S5_STATE, S5_ROW), BF16),
                   jax.ShapeDtypeStruct((g, 1, S5_STATE), F32),
                   jax.ShapeDtypeStruct((g, 1, S5_STATE), F32)],
        compiler_params=_params("parallel"),
        name="s5_operators",
    )(lam_re.reshape(g, 1, S5_STATE), lam_im.reshape(g, 1, S5_STATE), log_dt.reshape(g, 1, 1),
      b_re.transpose(0, 2, 1), b_im.transpose(0, 2, 1), c_re, c_im, d_row)


def _s5_ssm(xg, ops):
    _, chunks, bsz, _ = xg.shape
    rows = chunks * bsz
    n = S5_SSM_GROUPS_PER_STEP
    g_spec = lambda a, b: pl.BlockSpec((n, a, b), lambda g: (g, 0, 0))
    yg = pl.pallas_call(
        functools.partial(_s5_group_kernel, batch=bsz, chunks=chunks),
        grid=(S5_GROUPS // n,),
        in_specs=[g_spec(rows, S5_ROW), g_spec(S5_ROW, S5_ROW), g_spec(S5_ROW, 2 * S5_STATE),
                  g_spec(2 * S5_STATE, S5_ROW), g_spec(1, S5_STATE), g_spec(1, S5_STATE)],
        out_specs=g_spec(rows, S5_ROW),
        out_shape=jax.ShapeDtypeStruct((S5_GROUPS, rows, S5_ROW), F32),
        scratch_shapes=[pltpu.VMEM((n, rows, S5_STATE), F32)] * 4,
        compiler_params=_params("parallel"),
        name="s5_ssm",
    )(xg.reshape(S5_GROUPS, rows, S5_ROW), *ops)
    return yg.reshape(S5_GROUPS, chunks, bsz, S5_ROW)


def _residual_ple(mix, x, p, wo_ref, g_ref, wg_ref, bg_ref, wp_ref):
    x1 = x + _rmsnorm(_dot(mix, wo_ref[...]), g_ref[...])
    gate = _sigmoid(_dot(x1.astype(BF16), wg_ref[...]) + bg_ref[...])
    return x1 + _dot(p.astype(BF16), wp_ref[...]) * gate


def _fox_post_kernel(a_ref, x_ref, p_ref, wo_ref, g_ref, wg_ref, bg_ref, wp_ref, o_ref):
    o_ref[0] = _residual_ple(a_ref[0], x_ref[0], p_ref[0, 0], wo_ref, g_ref, wg_ref, bg_ref, wp_ref)


def _s5_post_kernel(yg_ref, z_ref, x_ref, p_ref, perm_ref, wglu_ref, bglu_ref, wo_ref, g_ref,
                    wg_ref, bg_ref, wp_ref, o_ref):
    batch, _, d = x_ref.shape
    slots = 128 // S5_GROUP
    slot = lax.broadcasted_iota(jnp.int32, (batch, 128), 1) >> 4
    tok = [[None] * (d // 128) for _ in range(S5_CHUNK)]
    for j in range(d // 128):
        for k in range(S5_CHUNK // slots):
            tiles = [yg_ref[slots * j + r, 0, :, k * 128:(k + 1) * 128] for r in range(slots)]
            for dist in (4, 2, 1):
                low = (slot & (2 * dist - 1)) < dist
                nxt = list(tiles)
                for r in range(slots):
                    if not r & dist:
                        a, b = tiles[r], tiles[r + dist]
                        nxt[r] = jnp.where(low, a, pltpu.roll(b, dist * S5_GROUP, 1))
                        nxt[r + dist] = jnp.where(low, pltpu.roll(a, 128 - dist * S5_GROUP, 1), b)
                tiles = nxt
            for tq in range(slots):
                tok[slots * k + tq][j] = tiles[tq]
    y = jnp.concatenate([jnp.concatenate(t, axis=1) for t in tok], axis=0)
    gl = _gelu_tanh(y)
    gl = gl * _sigmoid(_dot(gl.astype(BF16), wglu_ref[...]) + bglu_ref[...])
    mix = (gl * _silu(z_ref[0].astype(F32))).astype(BF16)
    mix = _dot(perm_ref[...], mix).astype(BF16)
    rows = batch * S5_CHUNK
    out = _residual_ple(mix, x_ref[...].reshape(rows, d), p_ref[0].reshape(rows, PLE_DIM),
                        wo_ref, g_ref, wg_ref, bg_ref, wp_ref)
    o_ref[...] = out.reshape(batch, S5_CHUNK, d)


def _tail_weights(w_out, g_post, w_gate, b_gate, w_proj):
    d = w_out.shape[0]
    args = [w_out.astype(BF16), g_post.reshape(1, d), w_gate.astype(BF16), b_gate.reshape(1, d),
            w_proj.astype(BF16)]
    return args, [_const_spec(a.shape) for a in args]


def _fox_post(gated, x, p, layer, *tail):
    bsz, seq, d = x.shape
    t = POST_ROWS
    row_spec = pl.BlockSpec((1, t, d), lambda b, i: (b, i, 0))
    p_spec = pl.BlockSpec((1, 1, t, PLE_DIM), lambda b, i: (layer, b, i, 0))
    w_args, w_specs = _tail_weights(*tail)
    return pl.pallas_call(
        _fox_post_kernel,
        grid=(bsz, seq // t),
        in_specs=[row_spec, row_spec, p_spec] + w_specs,
        out_specs=row_spec,
        out_shape=jax.ShapeDtypeStruct((bsz, seq, d), F32),
        compiler_params=_params("parallel", "parallel"),
        name="fox_post",
    )(gated, x, p, *w_args)


def _s5_post(yg, z, x, p, layer, w_glu, b_glu, *tail):
    bsz, seq, d = x.shape
    chunks = seq // S5_CHUNK
    rows = bsz * S5_CHUNK
    chunk_spec = pl.BlockSpec((bsz, S5_CHUNK, d), lambda c: (0, c, 0))
    p_spec = pl.BlockSpec((1, bsz, S5_CHUNK, PLE_DIM), lambda c: (layer, 0, c, 0))
    perm = jnp.asarray(_token_major_perm(bsz).T, BF16)
    w_args, w_specs = _tail_weights(*tail)
    return pl.pallas_call(
        _s5_post_kernel,
        grid=(chunks,),
        in_specs=[pl.BlockSpec((S5_GROUPS, 1, bsz, S5_ROW), lambda c: (0, c, 0, 0)),
                  pl.BlockSpec((1, rows, d), lambda c: (c, 0, 0)), chunk_spec, p_spec,
                  _const_spec(perm.shape), _const_spec((d, d)), _const_spec((1, d))] + w_specs,
        out_specs=chunk_spec,
        out_shape=jax.ShapeDtypeStruct((bsz, seq, d), F32),
        compiler_params=_params("parallel"),
        name="s5_post",
    )(yg, z, x, p, perm, w_glu.astype(BF16), b_glu.reshape(1, d), *w_args)


def kernel(x, p, ln_pre, ln_post, fox_w_in, fox_b_f, fox_w_out, s5_w_in, s5_lambda_re, s5_lambda_im,
           s5_log_dt, s5_b_re, s5_b_im, s5_c_re, s5_c_im, s5_d, s5_w_glu, s5_b_glu, s5_w_out,
           ple_w_proj, ple_w_gate, ple_b_gate):
    qa, ka, vt, z = _fox_proj(x, ln_pre[0], fox_w_in[0], fox_b_f[0])
    gated = _fox_attn(qa, ka, vt, z)
    x = _fox_post(gated, x, p, 0, fox_w_out[0], ln_post[0], ple_w_gate[0], ple_b_gate[0],
                  ple_w_proj[0])
    xg, z = _s5_front(x, ln_pre[1], s5_w_in[0])
    ops = _s5_operators(s5_lambda_re[0], s5_lambda_im[0], s5_log_dt[0], s5_b_re[0], s5_b_im[0],
                        s5_c_re[0], s5_c_im[0], s5_d[0])
    yg = _s5_ssm(xg, ops)
    return _s5_post(yg, z, x, p, 1, s5_w_glu[0], s5_b_glu[0], s5_w_out[0], ln_post[1],
                    ple_w_gate[1], ple_b_gate[1], ple_w_proj[1])
```

```python
import functools
import math

import jax
import jax.numpy as jnp
import numpy as np
from jax import lax
from jax.experimental import pallas as pl
from jax.experimental.pallas import tpu as pltpu

F32 = jnp.float32
BF16 = jnp.bfloat16

D_MODEL = 1024
PLE_DIM = 256
RMS_EPS = 1e-6
NEG_INF = -1e30

FOX_HEADS = 16
FOX_HEAD_DIM = 64
FOX_WIDTH = FOX_HEADS * FOX_HEAD_DIM
FOX_AUG_DIM = 128
FOX_AUG_WIDTH = FOX_HEADS * FOX_AUG_DIM

S5_GROUP = 16
S5_GROUPS = D_MODEL // S5_GROUP
S5_STATE = 64
S5_CHUNK = 16
S5_ROW = S5_CHUNK * S5_GROUP
S5_SSM_GROUPS_PER_STEP = 2

V7X_SCOPED_VMEM_LIMIT_BYTES = 56 * 1024 * 1024

PROJ_ROWS = 256
POST_ROWS = 512
ATTN_Q_ROWS = 256
ATTN_KV_ROWS = PROJ_ROWS
ATTN_SCORE_SLOTS = 3
LOG2_E = 1.4426950408889634


def _const_spec(shape):
    zeros = (0,) * len(shape)
    return pl.BlockSpec(shape, lambda *_: zeros, pipeline_mode=pl.Buffered(1))


def _params(*semantics):
    return pltpu.CompilerParams(dimension_semantics=semantics,
                                vmem_limit_bytes=V7X_SCOPED_VMEM_LIMIT_BYTES)


def _dot(a, b):
    return jnp.dot(a, b, preferred_element_type=F32)


def _rmsnorm(x, g):
    var = jnp.mean(x * x, axis=-1, keepdims=True)
    return x * lax.rsqrt(var + RMS_EPS) * g


def _sigmoid(x):
    return 1.0 / (1.0 + jnp.exp(-x))


def _silu(x):
    return x * _sigmoid(x)


def _gelu_tanh(x):
    c = math.sqrt(2.0 / math.pi)
    return 0.5 * x * (1.0 + jnp.tanh(c * (x + 0.044715 * (x * x * x))))


def _split3(x):
    hi = x.astype(BF16).astype(F32)
    r1 = x - hi
    mid = r1.astype(BF16).astype(F32)
    lo = (r1 - mid).astype(BF16).astype(F32)
    return hi, mid, lo


def _fox_proj_kernel(x_ref, g_ref, wq_ref, wk_ref, wv_ref, wz_ref, wf_ref, bf_ref, ek_ref,
                     q_out, k_out, v_out, z_out, carry_ref):
    rows = x_ref.shape[1]

    @pl.when(pl.program_id(1) == 0)
    def _():
        carry_ref[...] = jnp.zeros_like(carry_ref)

    h = _rmsnorm(x_ref[0], g_ref[...]).astype(BF16)

    zf = _dot(h, wf_ref[...]) + bf_ref[...]
    q = _dot(h, wq_ref[...]) * LOG2_E
    z_out[0] = _dot(h, wz_ref[...]).astype(BF16)
    log_f = jnp.minimum(zf, 0.0) - jnp.log(1.0 + jnp.exp(-jnp.abs(zf)))
    r = lax.broadcasted_iota(jnp.int32, (rows, rows), 0)
    c = lax.broadcasted_iota(jnp.int32, (rows, rows), 1)
    tri = jnp.where(r >= c, 1.0, 0.0).astype(BF16)
    hi, mid, lo = _split3(log_f)
    cum = (_dot(tri, hi.astype(BF16)) + _dot(tri, mid.astype(BF16))
           + _dot(tri, lo.astype(BF16))) + carry_ref[0:1, :]
    carry_ref[...] = jnp.broadcast_to(cum[rows - 1:rows, :], carry_ref.shape)

    c_hi, c_mid, c_lo = _split3(cum * LOG2_E)
    lane = lax.broadcasted_iota(jnp.int32, cum.shape, 1)
    aug = jnp.where(lane < 16, c_hi,
                    jnp.where(lane < 32, c_mid,
                              jnp.where(lane < 48, c_lo,
                                        jnp.where(lane < 51, 1.0, 0.0))))

    def widen(qk, upper):
        low = lax.broadcasted_iota(jnp.int32, (rows, FOX_AUG_DIM), 1) < FOX_HEAD_DIM
        tiles = []
        for j in range(FOX_HEADS // 2):
            pair = qk[:, j * FOX_AUG_DIM:(j + 1) * FOX_AUG_DIM]
            for a, val in enumerate((pair, pltpu.roll(pair, FOX_HEAD_DIM, 1))):
                n = (2 * j + a) * FOX_AUG_DIM
                up = upper if upper.shape[1] == FOX_AUG_DIM else upper[:, n:n + FOX_AUG_DIM]
                tiles.append(jnp.where(low, val, up))
        return jnp.concatenate(tiles, axis=1).astype(BF16)

    q_out[0] = widen(q, pltpu.roll(aug, FOX_HEAD_DIM, 1))
    k_out[0] = widen(_dot(h, wk_ref[...]), _dot(aug.astype(BF16), ek_ref[...]))
    v_out[0, 0] = lax.dot_general(wv_ref[...], h, (((1,), (1,)), ((), ())),
                                  preferred_element_type=F32).astype(BF16)


def _fox_k_placement():
    ek = np.zeros((128, FOX_AUG_WIDTH), np.float32)
    for h in range(FOX_HEADS):
        base = h * FOX_AUG_DIM + FOX_HEAD_DIM
        for part in range(3):
            ek[48, base + 16 * part + h] = 1.0
            ek[16 * part + h, base + 48 + part] = -1.0
    return jnp.asarray(ek, BF16)


def _fox_proj(x, g, w_in, b_f):
    bsz, seq, d = x.shape
    scale = FOX_HEAD_DIM ** -0.5
    wq = (w_in[:, :FOX_WIDTH] * scale).astype(BF16)
    wk = w_in[:, FOX_WIDTH:2 * FOX_WIDTH].astype(BF16)
    wv = w_in[:, 2 * FOX_WIDTH:3 * FOX_WIDTH].T.astype(BF16)
    wz = w_in[:, 3 * FOX_WIDTH:4 * FOX_WIDTH].astype(BF16)
    wf = w_in[:, 4 * FOX_WIDTH:]
    wf = jnp.pad(jnp.tile(wf, (1, 3)), ((0, 0), (0, 128 - 3 * FOX_HEADS))).astype(BF16)
    bf = jnp.pad(jnp.tile(b_f, 3), (0, 128 - 3 * FOX_HEADS)).reshape(1, 128).astype(F32)
    ek = _fox_k_placement()

    t = PROJ_ROWS
    row_spec = lambda w: pl.BlockSpec((1, t, w), lambda b, i: (b, i, 0))
    return pl.pallas_call(
        _fox_proj_kernel,
        grid=(bsz, seq // t),
        in_specs=[row_spec(d), _const_spec((1, d)),
                  _const_spec(wq.shape), _const_spec(wk.shape), _const_spec(wv.shape),
                  _const_spec(wz.shape), _const_spec(wf.shape), _const_spec(bf.shape),
                  _const_spec(ek.shape)],
        out_specs=[row_spec(FOX_AUG_WIDTH), row_spec(FOX_AUG_WIDTH),
                   pl.BlockSpec((1, 1, FOX_WIDTH, t), lambda b, i: (b, i, 0, 0)),
                   row_spec(FOX_WIDTH)],
        out_shape=[jax.ShapeDtypeStruct((bsz, seq, FOX_AUG_WIDTH), BF16),
                   jax.ShapeDtypeStruct((bsz, seq, FOX_AUG_WIDTH), BF16),
                   jax.ShapeDtypeStruct((bsz, seq // t, FOX_WIDTH, t), BF16),
                   jax.ShapeDtypeStruct((bsz, seq, FOX_WIDTH), BF16)],
        scratch_shapes=[pltpu.VMEM((8, 128), F32)],
        compiler_params=_params("parallel", "arbitrary"),
        name="fox_proj",
    )(x, g.reshape(1, d), wq, wk, wv, wz, wf, bf, ek)


def _fox_attn_kernel(q_ref, k_ref, vt_ref, z_ref, o_ref, st_ref, *, tq, tk):
    seq = q_ref.shape[1]
    heads = (0, 1)
    slots = st_ref.shape[0]
    steps = [(i, kb) for i in range(seq // tq) for kb in range(i + 1)]

    def issue_scores(t):
        i, kb = steps[t]
        for a in heads:
            lanes = slice(a * FOX_AUG_DIM, (a + 1) * FOX_AUG_DIM)
            k = k_ref[0, kb * tk:(kb + 1) * tk, lanes]
            q = q_ref[0, i * tq:(i + 1) * tq, lanes]
            st_ref[t % slots, a] = lax.dot_general(k, q, (((1,), (1,)), ((), ())),
                                                   preferred_element_type=F32)

    sum_rows = 16
    rid = lax.broadcasted_iota(jnp.int32, (sum_rows, tk), 0)
    ones_rows = jnp.where(rid == 0, 1.0, 0.0).astype(BF16)

    def update(a, kb, st, stats, masked):
        m, acc = stats
        if masked:
            kv = lax.broadcasted_iota(jnp.int32, st.shape, 0)
            qq = lax.broadcasted_iota(jnp.int32, st.shape, 1)
            st = jnp.where(kv <= qq, st, NEG_INF)
        m_new = jnp.maximum(m, jnp.max(st, axis=0, keepdims=True))
        alpha = jnp.exp2(m - m_new)
        p = jnp.exp2(st - m_new)
        vt = vt_ref[0, kb, a * FOX_HEAD_DIM:(a + 1) * FOX_HEAD_DIM, :]
        vt = jnp.concatenate([vt, ones_rows], axis=0)
        return m_new, alpha * acc + _dot(vt, p.astype(BF16))

    def init():
        return [(jnp.full((1, tq), NEG_INF, F32), jnp.zeros((FOX_HEAD_DIM + sum_rows, tq), F32))
                for _ in heads]

    for t in range(slots - 1):
        issue_scores(t)
    stats = init()
    for t, (i, kb) in enumerate(steps):
        if t + slots - 1 < len(steps):
            issue_scores(t + slots - 1)
        stats = [update(a, kb, st_ref[t % slots, a], stats[a], masked=(kb == i)) for a in heads]
        if kb == i:
            outs = [acc[:FOX_HEAD_DIM] / acc[FOX_HEAD_DIM:FOX_HEAD_DIM + 1] for _, acc in stats]
            o = jnp.concatenate(outs, axis=0).T
            rows = slice(i * tq, (i + 1) * tq)
            o_ref[0, rows, :] = (o * _silu(z_ref[0, rows, :].astype(F32))).astype(BF16)
            stats = init()


def _fox_attn(qa, ka, vt, z):
    bsz, seq, _ = z.shape
    tq = ATTN_Q_ROWS
    tk = ATTN_KV_ROWS
    assert tq == tk and vt.shape == (bsz, seq // tk, FOX_WIDTH, tk)
    pair = 2 * FOX_HEAD_DIM
    seq_spec = lambda w: pl.BlockSpec((1, seq, w), lambda b, j: (b, 0, j))
    return pl.pallas_call(
        functools.partial(_fox_attn_kernel, tq=tq, tk=tk),
        grid=(bsz, FOX_HEADS // 2),
        in_specs=[seq_spec(2 * FOX_AUG_DIM), seq_spec(2 * FOX_AUG_DIM),
                  pl.BlockSpec((1, seq // tk, pair, tk), lambda b, j: (b, 0, j, 0)),
                  seq_spec(pair)],
        out_specs=seq_spec(pair),
        out_shape=jax.ShapeDtypeStruct((bsz, seq, FOX_WIDTH), BF16),
        scratch_shapes=[pltpu.VMEM((ATTN_SCORE_SLOTS, 2, tk, tq), F32)],
        compiler_params=_params("parallel", "parallel"),
        name="fox_attn",
    )(qa, ka, vt, z)


def _slot_masks(batch):
    slot = lax.broadcasted_iota(jnp.int32, (batch, 128), 1) >> 4
    return [slot == q for q in range(128 // S5_GROUP)]


def _s5_front_kernel(x_ref, g_ref, perm_ref, wu_ref, wz_ref, xg_out, z_out):
    batch, _, d = x_ref.shape
    slots = 128 // S5_GROUP
    h = _rmsnorm(x_ref[...].reshape(batch * S5_CHUNK, d), g_ref[...]).astype(BF16)
    h = _dot(perm_ref[...], h).astype(BF16)
    u = _dot(h, wu_ref[...])
    z_out[0] = _dot(h, wz_ref[...]).astype(BF16)
    masks = _slot_masks(batch)
    rolled = []
    for s in range(S5_CHUNK):
        shift = (s % slots) * S5_GROUP
        tiles = [u[s * batch:(s + 1) * batch, j * 128:(j + 1) * 128] for j in range(d // 128)]
        rolled.append([pltpu.roll(t, shift, 1) if shift else t for t in tiles])
    for g in range(S5_GROUPS):
        j, r = divmod(g, slots)
        halves = []
        for k in range(S5_CHUNK // slots):
            acc = rolled[slots * k + (-r) % slots][j]
            for q in range(1, slots):
                acc = jnp.where(masks[q], rolled[slots * k + (q - r) % slots][j], acc)
            halves.append(acc)
        xg_out[g, 0] = jnp.concatenate(halves, axis=1).astype(BF16)


def _token_major_perm(bsz):
    n = np.arange(bsz * S5_CHUNK)
    perm = np.zeros((n.size, n.size), np.float32)
    perm[n, (n % bsz) * S5_CHUNK + n // bsz] = 1.0
    return perm


def _s5_front(x, g, w_in):
    bsz, seq, d = x.shape
    chunks = seq // S5_CHUNK
    rows = bsz * S5_CHUNK
    wu = w_in[:, :d].astype(BF16)
    wz = w_in[:, d:].astype(BF16)
    perm = jnp.asarray(_token_major_perm(bsz), BF16)
    return pl.pallas_call(
        _s5_front_kernel,
        grid=(chunks,),
        in_specs=[pl.BlockSpec((bsz, S5_CHUNK, d), lambda c: (0, c, 0)), _const_spec((1, d)),
                  _const_spec(perm.shape), _const_spec(wu.shape), _const_spec(wz.shape)],
        out_specs=[pl.BlockSpec((S5_GROUPS, 1, bsz, S5_ROW), lambda c: (0, c, 0, 0)),
                   pl.BlockSpec((1, rows, d), lambda c: (c, 0, 0))],
        out_shape=[jax.ShapeDtypeStruct((S5_GROUPS, chunks, bsz, S5_ROW), BF16),
                   jax.ShapeDtypeStruct((chunks, rows, d), BF16)],
        compiler_params=_params("parallel"),
        name="s5_front",
    )(x, g.reshape(1, d), perm, wu, wz)


def _s5_group_kernel(x_ref, w_ref, bst_ref, cst_ref, ar_ref, ai_ref, y_ref,
                     sre_ref, sim_ref, pre_ref, pim_ref, *, batch, chunks):
    groups = range(x_ref.shape[0])
    xs = [x_ref[g] for g in groups]
    for g in groups:
        s = _dot(xs[g], bst_ref[g])
        sre_ref[g] = s[:, :S5_STATE]
        sim_ref[g] = s[:, S5_STATE:]
    ar = [ar_ref[g] for g in groups]
    ai = [ai_ref[g] for g in groups]
    zero = jnp.zeros((batch, S5_STATE), F32)
    state = [(zero, zero) for _ in groups]
    for c in range(chunks):
        rows = slice(c * batch, (c + 1) * batch)
        for g in groups:
            xr, xi = state[g]
            pre_ref[g, rows, :] = xr
            pim_ref[g, rows, :] = xi
            state[g] = (ar[g] * xr - ai[g] * xi + sre_ref[g, rows, :],
                        ar[g] * xi + ai[g] * xr + sim_ref[g, rows, :])
    ys = [_dot(xs[g], w_ref[g]) for g in groups]
    for g in groups:
        prev = jnp.concatenate([pre_ref[g], pim_ref[g]], axis=1).astype(BF16)
        y_ref[g] = ys[g] + _dot(prev, cst_ref[g])


def _s5_operator_kernel(lre_ref, lim_ref, ldt_ref, brt_ref, bit_ref, cre_ref, cim_ref, d_ref,
                        w_ref, bst_ref, cst_ref, ar_ref, ai_ref):
    t = S5_CHUNK
    hp = lax.Precision.HIGHEST
    nt = (((1,), (1,)), ((), ()))
    tau = lax.broadcasted_iota(jnp.int32, (24, S5_STATE), 0).astype(F32)
    row = lax.broadcasted_iota(jnp.int32, (S5_GROUP, S5_ROW), 0)
    lane = lax.broadcasted_iota(jnp.int32, (S5_GROUP, S5_ROW), 1)
    for r in range(w_ref.shape[0]):
        lr, li = lre_ref[r], lim_ref[r]
        dt = jnp.exp(ldt_ref[r])
        mag = jnp.exp(lr * dt * tau)
        ang = li * dt * tau
        pw_re = mag * jnp.cos(ang)
        pw_im = mag * jnp.sin(ang)
        pr = [pw_re[n:n + 1] for n in range(t + 1)]
        pi = [pw_im[n:n + 1] for n in range(t + 1)]
        den = lr * lr + li * li
        num_re = pr[1] - 1.0
        num_im = pi[1]
        coef_re = (num_re * lr + num_im * li) / den
        coef_im = (num_im * lr - num_re * li) / den
        brt, bit = brt_ref[r], bit_ref[r]
        bb_re = coef_re * brt - coef_im * bit
        bb_im = coef_re * bit + coef_im * brt
        cr, ci = cre_ref[r], cim_ref[r]
        c_pw_re = [cr * pr[n] - ci * pi[n] for n in range(t + 1)]
        c_pw_im = [-(cr * pi[n] + ci * pr[n]) for n in range(t + 1)]
        kt = (lax.dot_general(bb_re, jnp.concatenate(c_pw_re[:t], axis=0), nt, precision=hp,
                              preferred_element_type=F32)
              + lax.dot_general(bb_im, jnp.concatenate(c_pw_im[:t], axis=0), nt, precision=hp,
                                preferred_element_type=F32))
        kt = kt + jnp.where(lane == row, d_ref[r], 0.0)
        for s in range(t):
            off = (8 * (s // 8) + (s % 8 + r) % 8) * S5_GROUP
            blk = kt if s == 0 else jnp.where(lane >= s * S5_GROUP,
                                              pltpu.roll(kt, s * S5_GROUP, 1), 0.0)
            w_ref[r, off:off + S5_GROUP, :] = blk.astype(BF16)
            n = t - 1 - s
            m_re = bb_re * pr[n] - bb_im * pi[n]
            m_im = bb_re * pi[n] + bb_im * pr[n]
            bst_ref[r, off:off + S5_GROUP, :] = jnp.concatenate([m_re, m_im], axis=1).astype(BF16)
        readout = jnp.concatenate([jnp.concatenate(c_pw_re[1:], axis=0),
                                   jnp.concatenate(c_pw_im[1:], axis=0)], axis=1)
        cst_ref[r] = readout.T.astype(BF16)
        ar_ref[r] = pr[t]
        ai_ref[r] = pi[t]


def _s5_operators(lam_re, lam_im, log_dt, b_re, b_im, c_re, c_im, d_skip):
    g = S5_GROUPS
    n = 128 // S5_GROUP
    spec = lambda a, b: pl.BlockSpec((n, a, b), lambda j: (j, 0, 0))
    d_row = jnp.pad(d_skip.reshape(g, 1, S5_GROUP), ((0, 0), (0, 0), (0, S5_ROW - S5_GROUP)))
    return pl.pallas_call(
        _s5_operator_kernel,
        grid=(g // n,),
        in_specs=[spec(1, S5_STATE), spec(1, S5_STATE), spec(1, 1), spec(S5_GROUP, S5_STATE),
                  spec(S5_GROUP, S5_STATE), spec(S5_GROUP, S5_STATE), spec(S5_GROUP, S5_STATE),
                  spec(1, S5_ROW)],
        out_specs=[spec(S5_ROW, S5_ROW), spec(S5_ROW, 2 * S5_STATE), spec(2 * S5_STATE, S5_ROW),
                   spec(1, S5_STATE), spec(1, S5_STATE)],
        out_shape=[jax.ShapeDtypeStruct((g, S5_ROW, S5_ROW), BF16),
                   jax.ShapeDtypeStruct((g, S5_ROW, 2 * S5_STATE), BF16),
                   jax.ShapeDtypeStruct((g, 2 * S5_STATE, S5_ROW), BF16),
                   jax.ShapeDtypeStruct((g, 1, S5_STATE), F32),
                   jax.ShapeDtypeStruct((g, 1, S5_STATE), F32)],
        compiler_params=_params("parallel"),
        name="s5_operators",
    )(lam_re.reshape(g, 1, S5_STATE), lam_im.reshape(g, 1, S5_STATE), log_dt.reshape(g, 1, 1),
      b_re.transpose(0, 2, 1), b_im.transpose(0, 2, 1), c_re, c_im, d_row)


def _s5_ssm(xg, ops):
    _, chunks, bsz, _ = xg.shape
    rows = chunks * bsz
    n = S5_SSM_GROUPS_PER_STEP
    g_spec = lambda a, b: pl.BlockSpec((n, a, b), lambda g: (g, 0, 0))
    yg = pl.pallas_call(
        functools.partial(_s5_group_kernel, batch=bsz, chunks=chunks),
        grid=(S5_GROUPS // n,),
        in_specs=[g_spec(rows, S5_ROW), g_spec(S5_ROW, S5_ROW), g_spec(S5_ROW, 2 * S5_STATE),
                  g_spec(2 * S5_STATE, S5_ROW), g_spec(1, S5_STATE), g_spec(1, S5_STATE)],
        out_specs=g_spec(rows, S5_ROW),
        out_shape=jax.ShapeDtypeStruct((S5_GROUPS, rows, S5_ROW), F32),
        scratch_shapes=[pltpu.VMEM((n, rows, S5_STATE), F32)] * 4,
        compiler_params=_params("parallel"),
        name="s5_ssm",
    )(xg.reshape(S5_GROUPS, rows, S5_ROW), *ops)
    return yg.reshape(S5_GROUPS, chunks, bsz, S5_ROW)


def _residual_ple(mix, x, p, wo_ref, g_ref, wg_ref, bg_ref, wp_ref):
    x1 = x + _rmsnorm(_dot(mix, wo_ref[...]), g_ref[...])
    gate = _sigmoid(_dot(x1.astype(BF16), wg_ref[...]) + bg_ref[...])
    return x1 + _dot(p.astype(BF16), wp_ref[...]) * gate


def _fox_post_kernel(a_ref, x_ref, p_ref, wo_ref, g_ref, wg_ref, bg_ref, wp_ref, o_ref):
    o_ref[0] = _residual_ple(a_ref[0], x_ref[0], p_ref[0, 0], wo_ref, g_ref, wg_ref, bg_ref, wp_ref)


def _s5_post_kernel(yg_ref, z_ref, x_ref, p_ref, perm_ref, wglu_ref, bglu_ref, wo_ref, g_ref,
                    wg_ref, bg_ref, wp_ref, o_ref):
    batch, _, d = x_ref.shape
    slots = 128 // S5_GROUP
    slot = lax.broadcasted_iota(jnp.int32, (batch, 128), 1) >> 4
    tok = [[None] * (d // 128) for _ in range(S5_CHUNK)]
    for j in range(d // 128):
        for k in range(S5_CHUNK // slots):
            tiles = [yg_ref[slots * j + r, 0, :, k * 128:(k + 1) * 128] for r in range(slots)]
            for dist in (4, 2, 1):
                low = (slot & (2 * dist - 1)) < dist
                nxt = list(tiles)
                for r in range(slots):
                    if not r & dist:
                        a, b = tiles[r], tiles[r + dist]
                        nxt[r] = jnp.where(low, a, pltpu.roll(b, dist * S5_GROUP, 1))
                        nxt[r + dist] = jnp.where(low, pltpu.roll(a, 128 - dist * S5_GROUP, 1), b)
                tiles = nxt
            for tq in range(slots):
                tok[slots * k + tq][j] = tiles[tq]
    y = jnp.concatenate([jnp.concatenate(t, axis=1) for t in tok], axis=0)
    gl = _gelu_tanh(y)
    gl = gl * _sigmoid(_dot(gl.astype(BF16), wglu_ref[...]) + bglu_ref[...])
    mix = (gl * _silu(z_ref[0].astype(F32))).astype(BF16)
    mix = _dot(perm_ref[...], mix).astype(BF16)
    rows = batch * S5_CHUNK
    out = _residual_ple(mix, x_ref[...].reshape(rows, d), p_ref[0].reshape(rows, PLE_DIM),
                        wo_ref, g_ref, wg_ref, bg_ref, wp_ref)
    o_ref[...] = out.reshape(batch, S5_CHUNK, d)


def _tail_weights(w_out, g_post, w_gate, b_gate, w_proj):
    d = w_out.shape[0]
    args = [w_out.astype(BF16), g_post.reshape(1, d), w_gate.astype(BF16), b_gate.reshape(1, d),
            w_proj.astype(BF16)]
    return args, [_const_spec(a.shape) for a in args]


def _fox_post(gated, x, p, layer, *tail):
    bsz, seq, d = x.shape
    t = POST_ROWS
    row_spec = pl.BlockSpec((1, t, d), lambda b, i: (b, i, 0))
    p_spec = pl.BlockSpec((1, 1, t, PLE_DIM), lambda b, i: (layer, b, i, 0))
    w_args, w_specs = _tail_weights(*tail)
    return pl.pallas_call(
        _fox_post_kernel,
        grid=(bsz, seq // t),
        in_specs=[row_spec, row_spec, p_spec] + w_specs,
        out_specs=row_spec,
        out_shape=jax.ShapeDtypeStruct((bsz, seq, d), F32),
        compiler_params=_params("parallel", "parallel"),
        name="fox_post",
    )(gated, x, p, *w_args)


def _s5_post(yg, z, x, p, layer, w_glu, b_glu, *tail):
    bsz, seq, d = x.shape
    chunks = seq // S5_CHUNK
    rows = bsz * S5_CHUNK
    chunk_spec = pl.BlockSpec((bsz, S5_CHUNK, d), lambda c: (0, c, 0))
    p_spec = pl.BlockSpec((1, bsz, S5_CHUNK, PLE_DIM), lambda c: (layer, 0, c, 0))
    perm = jnp.asarray(_token_major_perm(bsz).T, BF16)
    w_args, w_specs = _tail_weights(*tail)
    return pl.pallas_call(
        _s5_post_kernel,
        grid=(chunks,),
        in_specs=[pl.BlockSpec((S5_GROUPS, 1, bsz, S5_ROW), lambda c: (0, c, 0, 0)),
                  pl.BlockSpec((1, rows, d), lambda c: (c, 0, 0)), chunk_spec, p_spec,
                  _const_spec(perm.shape), _const_spec((d, d)), _const_spec((1, d))] + w_specs,
        out_specs=chunk_spec,
        out_shape=jax.ShapeDtypeStruct((bsz, seq, d), F32),
        compiler_params=_params("parallel"),
        name="s5_post",
    )(yg, z, x, p, perm, w_glu.astype(BF16), b_glu.reshape(1, d), *w_args)


def kernel(x, p, ln_pre, ln_post, fox_w_in, fox_b_f, fox_w_out, s5_w_in, s5_lambda_re, s5_lambda_im,
           s5_log_dt, s5_b_re, s5_b_im, s5_c_re, s5_c_im, s5_d, s5_w_glu, s5_b_glu, s5_w_out,
           ple_w_proj, ple_w_gate, ple_b_gate):
    qa, ka, vt, z = _fox_proj(x, ln_pre[0], fox_w_in[0], fox_b_f[0])
    gated = _fox_attn(qa, ka, vt, z)
    x = _fox_post(gated, x, p, 0, fox_w_out[0], ln_post[0], ple_w_gate[0], ple_b_gate[0],
                  ple_w_proj[0])
    xg, z = _s5_front(x, ln_pre[1], s5_w_in[0])
    ops = _s5_operators(s5_lambda_re[0], s5_lambda_im[0], s5_log_dt[0], s5_b_re[0], s5_b_im[0],
                        s5_c_re[0], s5_c_im[0], s5_d[0])
    yg = _s5_ssm(xg, ops)
    return _s5_post(yg, z, x, p, 1, s5_w_glu[0], s5_b_glu[0], s5_w_out[0], ln_post[1],
                    ple_w_gate[1], ple_b_gate[1], ple_w_proj[1])
```

```python
import functools
import math

import jax
import jax.numpy as jnp
import numpy as np
from jax import lax
from jax.experimental import pallas as pl
from jax.experimental.pallas import tpu as pltpu

F32 = jnp.float32
BF16 = jnp.bfloat16

D_MODEL = 1024
PLE_DIM = 256
RMS_EPS = 1e-6
NEG_INF = -1e30

FOX_HEADS = 16
FOX_HEAD_DIM = 64
FOX_WIDTH = FOX_HEADS * FOX_HEAD_DIM
FOX_AUG_DIM = 128
FOX_AUG_WIDTH = FOX_HEADS * FOX_AUG_DIM

S5_GROUP = 16
S5_GROUPS = D_MODEL // S5_GROUP
S5_STATE = 64
S5_CHUNK = 16
S5_ROW = S5_CHUNK * S5_GROUP
S5_SSM_GROUPS_PER_STEP = 2

V7X_SCOPED_VMEM_LIMIT_BYTES = 56 * 1024 * 1024

PROJ_ROWS = 512
POST_ROWS = 1024
ATTN_Q_ROWS = 256
ATTN_KV_ROWS = 256
ATTN_SCORE_SLOTS = 3
LOG2_E = 1.4426950408889634


def _const_spec(shape):
    zeros = (0,) * len(shape)
    return pl.BlockSpec(shape, lambda *_: zeros, pipeline_mode=pl.Buffered(1))


def _params(*semantics):
    return pltpu.CompilerParams(dimension_semantics=semantics,
                                vmem_limit_bytes=V7X_SCOPED_VMEM_LIMIT_BYTES)


def _dot(a, b):
    return jnp.dot(a, b, preferred_element_type=F32)


def _rmsnorm(x, g):
    var = jnp.mean(x * x, axis=-1, keepdims=True)
    return x * lax.rsqrt(var + RMS_EPS) * g


def _sigmoid(x):
    return 1.0 / (1.0 + jnp.exp(-x))


def _silu(x):
    return x * _sigmoid(x)


def _gelu_tanh(x):
    c = math.sqrt(2.0 / math.pi)
    return 0.5 * x * (1.0 + jnp.tanh(c * (x + 0.044715 * (x * x * x))))


def _split3(x):
    hi = x.astype(BF16).astype(F32)
    r1 = x - hi
    mid = r1.astype(BF16).astype(F32)
    lo = (r1 - mid).astype(BF16).astype(F32)
    return hi, mid, lo


def _fox_proj_kernel(x_ref, g_ref, wq_ref, wk_ref, wv_ref, wz_ref, wf_ref, bf_ref, ek_ref,
                     q_out, k_out, v_out, z_out, carry_ref):
    rows = x_ref.shape[1]

    @pl.when(pl.program_id(1) == 0)
    def _():
        carry_ref[...] = jnp.zeros_like(carry_ref)

    h = _rmsnorm(x_ref[0], g_ref[...]).astype(BF16)

    zf = _dot(h, wf_ref[...]) + bf_ref[...]
    q = _dot(h, wq_ref[...]) * LOG2_E
    z_out[0] = _dot(h, wz_ref[...]).astype(BF16)
    log_f = jnp.minimum(zf, 0.0) - jnp.log(1.0 + jnp.exp(-jnp.abs(zf)))
    r = lax.broadcasted_iota(jnp.int32, (rows, rows), 0)
    c = lax.broadcasted_iota(jnp.int32, (rows, rows), 1)
    tri = jnp.where(r >= c, 1.0, 0.0).astype(BF16)
    hi, mid, lo = _split3(log_f)
    cum = (_dot(tri, hi.astype(BF16)) + _dot(tri, mid.astype(BF16))
           + _dot(tri, lo.astype(BF16))) + carry_ref[0:1, :]
    carry_ref[...] = jnp.broadcast_to(cum[rows - 1:rows, :], carry_ref.shape)

    c_hi, c_mid, c_lo = _split3(cum * LOG2_E)
    lane = lax.broadcasted_iota(jnp.int32, cum.shape, 1)
    aug = jnp.where(lane < 16, c_hi,
                    jnp.where(lane < 32, c_mid,
                              jnp.where(lane < 48, c_lo,
                                        jnp.where(lane < 51, 1.0, 0.0))))

    def widen(qk, upper):
        low = lax.broadcasted_iota(jnp.int32, (rows, FOX_AUG_DIM), 1) < FOX_HEAD_DIM
        tiles = []
        for j in range(FOX_HEADS // 2):
            pair = qk[:, j * FOX_AUG_DIM:(j + 1) * FOX_AUG_DIM]
            for a, val in enumerate((pair, pltpu.roll(pair, FOX_HEAD_DIM, 1))):
                n = (2 * j + a) * FOX_AUG_DIM
                up = upper if upper.shape[1] == FOX_AUG_DIM else upper[:, n:n + FOX_AUG_DIM]
                tiles.append(jnp.where(low, val, up))
        return jnp.concatenate(tiles, axis=1).astype(BF16)

    q_out[0] = widen(q, pltpu.roll(aug, FOX_HEAD_DIM, 1))
    k_out[0] = widen(_dot(h, wk_ref[...]), _dot(aug.astype(BF16), ek_ref[...]))
    vt = lax.dot_general(wv_ref[...], h, (((1,), (1,)), ((), ())),
                         preferred_element_type=F32).astype(BF16)
    for n in range(v_out.shape[1]):
        v_out[0, n] = vt[:, n * ATTN_KV_ROWS:(n + 1) * ATTN_KV_ROWS]


def _fox_k_placement():
    ek = np.zeros((128, FOX_AUG_WIDTH), np.float32)
    for h in range(FOX_HEADS):
        base = h * FOX_AUG_DIM + FOX_HEAD_DIM
        for part in range(3):
            ek[48, base + 16 * part + h] = 1.0
            ek[16 * part + h, base + 48 + part] = -1.0
    return jnp.asarray(ek, BF16)


def _fox_proj(x, g, w_in, b_f):
    bsz, seq, d = x.shape
    scale = FOX_HEAD_DIM ** -0.5
    wq = (w_in[:, :FOX_WIDTH] * scale).astype(BF16)
    wk = w_in[:, FOX_WIDTH:2 * FOX_WIDTH].astype(BF16)
    wv = w_in[:, 2 * FOX_WIDTH:3 * FOX_WIDTH].T.astype(BF16)
    wz = w_in[:, 3 * FOX_WIDTH:4 * FOX_WIDTH].astype(BF16)
    wf = w_in[:, 4 * FOX_WIDTH:]
    wf = jnp.pad(jnp.tile(wf, (1, 3)), ((0, 0), (0, 128 - 3 * FOX_HEADS))).astype(BF16)
    bf = jnp.pad(jnp.tile(b_f, 3), (0, 128 - 3 * FOX_HEADS)).reshape(1, 128).astype(F32)
    ek = _fox_k_placement()

    t = PROJ_ROWS
    tk = ATTN_KV_ROWS
    row_spec = lambda w: pl.BlockSpec((1, t, w), lambda b, i: (b, i, 0))
    return pl.pallas_call(
        _fox_proj_kernel,
        grid=(bsz, seq // t),
        in_specs=[row_spec(d), _const_spec((1, d)),
                  _const_spec(wq.shape), _const_spec(wk.shape), _const_spec(wv.shape),
                  _const_spec(wz.shape), _const_spec(wf.shape), _const_spec(bf.shape),
                  _const_spec(ek.shape)],
        out_specs=[row_spec(FOX_AUG_WIDTH), row_spec(FOX_AUG_WIDTH),
                   pl.BlockSpec((1, t // tk, FOX_WIDTH, tk), lambda b, i: (b, i, 0, 0)),
                   row_spec(FOX_WIDTH)],
        out_shape=[jax.ShapeDtypeStruct((bsz, seq, FOX_AUG_WIDTH), BF16),
                   jax.ShapeDtypeStruct((bsz, seq, FOX_AUG_WIDTH), BF16),
                   jax.ShapeDtypeStruct((bsz, seq // tk, FOX_WIDTH, tk), BF16),
                   jax.ShapeDtypeStruct((bsz, seq, FOX_WIDTH), BF16)],
        scratch_shapes=[pltpu.VMEM((8, 128), F32)],
        compiler_params=_params("parallel", "arbitrary"),
        name="fox_proj",
    )(x, g.reshape(1, d), wq, wk, wv, wz, wf, bf, ek)


def _fox_attn_kernel(q_ref, k_ref, vt_ref, z_ref, o_ref, st_ref, *, tq, tk):
    seq = q_ref.shape[1]
    heads = (0, 1)
    slots = st_ref.shape[0]
    steps = [(i, kb) for i in range(seq // tq) for kb in range(i + 1)]

    def issue_scores(t):
        i, kb = steps[t]
        for a in heads:
            lanes = slice(a * FOX_AUG_DIM, (a + 1) * FOX_AUG_DIM)
            k = k_ref[0, kb * tk:(kb + 1) * tk, lanes]
            q = q_ref[0, i * tq:(i + 1) * tq, lanes]
            st_ref[t % slots, a] = lax.dot_general(k, q, (((1,), (1,)), ((), ())),
                                                   preferred_element_type=F32)

    sum_rows = 16
    rid = lax.broadcasted_iota(jnp.int32, (sum_rows, tk), 0)
    ones_rows = jnp.where(rid == 0, 1.0, 0.0).astype(BF16)

    def update(a, kb, st, stats, masked):
        m, acc = stats
        if masked:
            kv = lax.broadcasted_iota(jnp.int32, st.shape, 0)
            qq = lax.broadcasted_iota(jnp.int32, st.shape, 1)
            st = jnp.where(kv <= qq, st, NEG_INF)
        m_new = jnp.maximum(m, jnp.max(st, axis=0, keepdims=True))
        alpha = jnp.exp2(m - m_new)
        p = jnp.exp2(st - m_new)
        vt = vt_ref[0, kb, a * FOX_HEAD_DIM:(a + 1) * FOX_HEAD_DIM, :]
        vt = jnp.concatenate([vt, ones_rows], axis=0)
        return m_new, alpha * acc + _dot(vt, p.astype(BF16))

    def init():
        return [(jnp.full((1, tq), NEG_INF, F32), jnp.zeros((FOX_HEAD_DIM + sum_rows, tq), F32))
                for _ in heads]

    for t in range(slots - 1):
        issue_scores(t)
    stats = init()
    for t, (i, kb) in enumerate(steps):
        if t + slots - 1 < len(steps):
            issue_scores(t + slots - 1)
        stats = [update(a, kb, st_ref[t % slots, a], stats[a], masked=(kb == i)) for a in heads]
        if kb == i:
            outs = [acc[:FOX_HEAD_DIM] / acc[FOX_HEAD_DIM:FOX_HEAD_DIM + 1] for _, acc in stats]
            o = jnp.concatenate(outs, axis=0).T
            rows = slice(i * tq, (i + 1) * tq)
            o_ref[0, rows, :] = (o * _silu(z_ref[0, rows, :].astype(F32))).astype(BF16)
            stats = init()


def _fox_attn(qa, ka, vt, z):
    bsz, seq, _ = z.shape
    tq = ATTN_Q_ROWS
    tk = ATTN_KV_ROWS
    assert tq == tk and vt.shape == (bsz, seq // tk, FOX_WIDTH, tk)
    pair = 2 * FOX_HEAD_DIM
    seq_spec = lambda w: pl.BlockSpec((1, seq, w), lambda b, j: (b, 0, j))
    return pl.pallas_call(
        functools.partial(_fox_attn_kernel, tq=tq, tk=tk),
        grid=(bsz, FOX_HEADS // 2),
        in_specs=[seq_spec(2 * FOX_AUG_DIM), seq_spec(2 * FOX_AUG_DIM),
                  pl.BlockSpec((1, seq // tk, pair, tk), lambda b, j: (b, 0, j, 0)),
                  seq_spec(pair)],
        out_specs=seq_spec(pair),
        out_shape=jax.ShapeDtypeStruct((bsz, seq, FOX_WIDTH), BF16),
        scratch_shapes=[pltpu.VMEM((ATTN_SCORE_SLOTS, 2, tk, tq), F32)],
        compiler_params=_params("parallel", "parallel"),
        name="fox_attn",
    )(qa, ka, vt, z)


def _slot_masks(batch):
    slot = lax.broadcasted_iota(jnp.int32, (batch, 128), 1) >> 4
    return [slot == q for q in range(128 // S5_GROUP)]


def _s5_front_kernel(x_ref, g_ref, perm_ref, wu_ref, wz_ref, xg_out, z_out):
    batch, _, d = x_ref.shape
    slots = 128 // S5_GROUP
    h = _rmsnorm(x_ref[...].reshape(batch * S5_CHUNK, d), g_ref[...]).astype(BF16)
    h = _dot(perm_ref[...], h).astype(BF16)
    u = _dot(h, wu_ref[...])
    z_out[0] = _dot(h, wz_ref[...]).astype(BF16)
    masks = _slot_masks(batch)
    rolled = []
    for s in range(S5_CHUNK):
        shift = (s % slots) * S5_GROUP
        tiles = [u[s * batch:(s + 1) * batch, j * 128:(j + 1) * 128] for j in range(d // 128)]
        rolled.append([pltpu.roll(t, shift, 1) if shift else t for t in tiles])
    for g in range(S5_GROUPS):
        j, r = divmod(g, slots)
        halves = []
        for k in range(S5_CHUNK // slots):
            acc = rolled[slots * k + (-r) % slots][j]
            for q in range(1, slots):
                acc = jnp.where(masks[q], rolled[slots * k + (q - r) % slots][j], acc)
            halves.append(acc)
        xg_out[g, 0] = jnp.concatenate(halves, axis=1).astype(BF16)


def _token_major_perm(bsz):
    n = np.arange(bsz * S5_CHUNK)
    perm = np.zeros((n.size, n.size), np.float32)
    perm[n, (n % bsz) * S5_CHUNK + n // bsz] = 1.0
    return perm


def _s5_front(x, g, w_in):
    bsz, seq, d = x.shape
    chunks = seq // S5_CHUNK
    rows = bsz * S5_CHUNK
    wu = w_in[:, :d].astype(BF16)
    wz = w_in[:, d:].astype(BF16)
    perm = jnp.asarray(_token_major_perm(bsz), BF16)
    return pl.pallas_call(
        _s5_front_kernel,
        grid=(chunks,),
        in_specs=[pl.BlockSpec((bsz, S5_CHUNK, d), lambda c: (0, c, 0)), _const_spec((1, d)),
                  _const_spec(perm.shape), _const_spec(wu.shape), _const_spec(wz.shape)],
        out_specs=[pl.BlockSpec((S5_GROUPS, 1, bsz, S5_ROW), lambda c: (0, c, 0, 0)),
                   pl.BlockSpec((1, rows, d), lambda c: (c, 0, 0))],
        out_shape=[jax.ShapeDtypeStruct((S5_GROUPS, chunks, bsz, S5_ROW), BF16),
                   jax.ShapeDtypeStruct((chunks, rows, d), BF16)],
        compiler_params=_params("parallel"),
        name="s5_front",
    )(x, g.reshape(1, d), perm, wu, wz)


def _s5_group_kernel(x_ref, w_ref, bst_ref, cst_ref, ar_ref, ai_ref, y_ref,
                     sre_ref, sim_ref, pre_ref, pim_ref, *, batch, chunks):
    groups = range(x_ref.shape[0])
    xs = [x_ref[g] for g in groups]
    for g in groups:
        s = _dot(xs[g], bst_ref[g])
        sre_ref[g] = s[:, :S5_STATE]
        sim_ref[g] = s[:, S5_STATE:]
    ar = [ar_ref[g] for g in groups]
    ai = [ai_ref[g] for g in groups]
    zero = jnp.zeros((batch, S5_STATE), F32)
    state = [(zero, zero) for _ in groups]
    for c in range(chunks):
        rows = slice(c * batch, (c + 1) * batch)
        for g in groups:
            xr, xi = state[g]
            pre_ref[g, rows, :] = xr
            pim_ref[g, rows, :] = xi
            state[g] = (ar[g] * xr - ai[g] * xi + sre_ref[g, rows, :],
                        ar[g] * xi + ai[g] * xr + sim_ref[g, rows, :])
    ys = [_dot(xs[g], w_ref[g]) for g in groups]
    for g in groups:
        prev = jnp.concatenate([pre_ref[g], pim_ref[g]], axis=1).astype(BF16)
        y_ref[g] = ys[g] + _dot(prev, cst_ref[g])


def _s5_operator_kernel(lre_ref, lim_ref, ldt_ref, brt_ref, bit_ref, cre_ref, cim_ref, d_ref,
                        w_ref, bst_ref, cst_ref, ar_ref, ai_ref):
    t = S5_CHUNK
    hp = lax.Precision.HIGHEST
    nt = (((1,), (1,)), ((), ()))
    tau = lax.broadcasted_iota(jnp.int32, (24, S5_STATE), 0).astype(F32)
    row = lax.broadcasted_iota(jnp.int32, (S5_GROUP, S5_ROW), 0)
    lane = lax.broadcasted_iota(jnp.int32, (S5_GROUP, S5_ROW), 1)
    for r in range(w_ref.shape[0]):
        lr, li = lre_ref[r], lim_ref[r]
        dt = jnp.exp(ldt_ref[r])
        mag = jnp.exp(lr * dt * tau)
        ang = li * dt * tau
        pw_re = mag * jnp.cos(ang)
        pw_im = mag * jnp.sin(ang)
        pr = [pw_re[n:n + 1] for n in range(t + 1)]
        pi = [pw_im[n:n + 1] for n in range(t + 1)]
        den = lr * lr + li * li
        num_re = pr[1] - 1.0
        num_im = pi[1]
        coef_re = (num_re * lr + num_im * li) / den
        coef_im = (num_im * lr - num_re * li) / den
        brt, bit = brt_ref[r], bit_ref[r]
        bb_re = coef_re * brt - coef_im * bit
        bb_im = coef_re * bit + coef_im * brt
        cr, ci = cre_ref[r], cim_ref[r]
        c_pw_re = [cr * pr[n] - ci * pi[n] for n in range(t + 1)]
        c_pw_im = [-(cr * pi[n] + ci * pr[n]) for n in range(t + 1)]
        kt = (lax.dot_general(bb_re, jnp.concatenate(c_pw_re[:t], axis=0), nt, precision=hp,
                              preferred_element_type=F32)
              + lax.dot_general(bb_im, jnp.concatenate(c_pw_im[:t], axis=0), nt, precision=hp,
                                preferred_element_type=F32))
        kt = kt + jnp.where(lane == row, d_ref[r], 0.0)
        for s in range(t):
            off = (8 * (s // 8) + (s % 8 + r) % 8) * S5_GROUP
            blk = kt if s == 0 else jnp.where(lane >= s * S5_GROUP,
                                              pltpu.roll(kt, s * S5_GROUP, 1), 0.0)
            w_ref[r, off:off + S5_GROUP, :] = blk.astype(BF16)
            n = t - 1 - s
            m_re = bb_re * pr[n] - bb_im * pi[n]
            m_im = bb_re * pi[n] + bb_im * pr[n]
            bst_ref[r, off:off + S5_GROUP, :] = jnp.concatenate([m_re, m_im], axis=1).astype(BF16)
        readout = jnp.concatenate([jnp.concatenate(c_pw_re[1:], axis=0),
                                   jnp.concatenate(c_pw_im[1:], axis=0)], axis=1)
        cst_ref[r] = readout.T.astype(BF16)
        ar_ref[r] = pr[t]
        ai_ref[r] = pi[t]


def _s5_operators(lam_re, lam_im, log_dt, b_re, b_im, c_re, c_im, d_skip):
    g = S5_GROUPS
    n = 128 // S5_GROUP
    spec = lambda a, b: pl.BlockSpec((n, a, b), lambda j: (j, 0, 0))
    d_row = jnp.pad(d_skip.reshape(g, 1, S5_GROUP), ((0, 0), (0, 0), (0, S5_ROW - S5_GROUP)))
    return pl.pallas_call(
        _s5_operator_kernel,
        grid=(g // n,),
        in_specs=[spec(1, S5_STATE), spec(1, S5_STATE), spec(1, 1), spec(S5_GROUP, S5_STATE),
                  spec(S5_GROUP, S5_STATE), spec(S5_GROUP, S5_STATE), spec(S5_GROUP, S5_STATE),
                  spec(1, S5_ROW)],
        out_specs=[spec(S5_ROW, S5_ROW), spec(S5_ROW, 2 * S5_STATE), spec(2 * S5_STATE, S5_ROW),
                   spec(1, S5_STATE), spec(1, S5_STATE)],
        out_shape=[jax.ShapeDtypeStruct((g, S5_ROW, S5_ROW), BF16),
                   jax.ShapeDtypeStruct((g, S5_ROW, 2 * S5_STATE), BF16),
                   jax.ShapeDtypeStruct((g, 2 * S5_STATE, S5_ROW), BF16),
                   jax.ShapeDtypeStruct((g, 1, S5_STATE), F32),
                   jax.ShapeDtypeStruct((g, 1, S5_STATE), F32)],
        compiler_params=_params("parallel"),
        name="s5_operators",
    )(lam_re.reshape(g, 1, S5_STATE), lam_im.reshape(g, 1, S5_STATE), log_dt.reshape(g, 1, 1),
      b_re.transpose(0, 2, 1), b_im.transpose(0, 2, 1), c_re, c_im, d_row)


def _s5_ssm(xg, ops):
    _, chunks, bsz, _ = xg.shape
    rows = chunks * bsz
    n = S5_SSM_GROUPS_PER_STEP
    g_spec = lambda a, b: pl.BlockSpec((n, a, b), lambda g: (g, 0, 0))
    yg = pl.pallas_call(
        functools.partial(_s5_group_kernel, batch=bsz, chunks=chunks),
        grid=(S5_GROUPS // n,),
        in_specs=[g_spec(rows, S5_ROW), g_spec(S5_ROW, S5_ROW), g_spec(S5_ROW, 2 * S5_STATE),
                  g_spec(2 * S5_STATE, S5_ROW), g_spec(1, S5_STATE), g_spec(1, S5_STATE)],
        out_specs=g_spec(rows, S5_ROW),
        out_shape=jax.ShapeDtypeStruct((S5_GROUPS, rows, S5_ROW), F32),
        scratch_shapes=[pltpu.VMEM((n, rows, S5_STATE), F32)] * 4,
        compiler_params=_params("parallel"),
        name="s5_ssm",
    )(xg.reshape(S5_GROUPS, rows, S5_ROW), *ops)
    return yg.reshape(S5_GROUPS, chunks, bsz, S5_ROW)


def _residual_ple(mix, x, p, wo_ref, g_ref, wg_ref, bg_ref, wp_ref):
    x1 = x + _rmsnorm(_dot(mix, wo_ref[...]), g_ref[...])
    gate = _sigmoid(_dot(x1.astype(BF16), wg_ref[...]) + bg_ref[...])
    return x1 + _dot(p.astype(BF16), wp_ref[...]) * gate


def _fox_post_kernel(a_ref, x_ref, p_ref, wo_ref, g_ref, wg_ref, bg_ref, wp_ref, o_ref):
    o_ref[0] = _residual_ple(a_ref[0], x_ref[0], p_ref[0, 0], wo_ref, g_ref, wg_ref, bg_ref, wp_ref)


def _s5_post_kernel(yg_ref, z_ref, x_ref, p_ref, perm_ref, wglu_ref, bglu_ref, wo_ref, g_ref,
                    wg_ref, bg_ref, wp_ref, o_ref):
    batch, _, d = x_ref.shape
    slots = 128 // S5_GROUP
    slot = lax.broadcasted_iota(jnp.int32, (batch, 128), 1) >> 4
    tok = [[None] * (d // 128) for _ in range(S5_CHUNK)]
    for j in range(d // 128):
        for k in range(S5_CHUNK // slots):
            tiles = [yg_ref[slots * j + r, 0, :, k * 128:(k + 1) * 128] for r in range(slots)]
            for dist in (4, 2, 1):
                low = (slot & (2 * dist - 1)) < dist
                nxt = list(tiles)
                for r in range(slots):
                    if not r & dist:
                        a, b = tiles[r], tiles[r + dist]
                        nxt[r] = jnp.where(low, a, pltpu.roll(b, dist * S5_GROUP, 1))
                        nxt[r + dist] = jnp.where(low, pltpu.roll(a, 128 - dist * S5_GROUP, 1), b)
                tiles = nxt
            for tq in range(slots):
                tok[slots * k + tq][j] = tiles[tq]
    y = jnp.concatenate([jnp.concatenate(t, axis=1) for t in tok], axis=0)
    gl = _gelu_tanh(y)
    gl = gl * _sigmoid(_dot(gl.astype(BF16), wglu_ref[...]) + bglu_ref[...])
    mix = (gl * _silu(z_ref[0].astype(F32))).astype(BF16)
    mix = _dot(perm_ref[...], mix).astype(BF16)
    rows = batch * S5_CHUNK
    out = _residual_ple(mix, x_ref[...].reshape(rows, d), p_ref[0].reshape(rows, PLE_DIM),
                        wo_ref, g_ref, wg_ref, bg_ref, wp_ref)
    o_ref[...] = out.reshape(batch, S5_CHUNK, d)


def _tail_weights(w_out, g_post, w_gate, b_gate, w_proj):
    d = w_out.shape[0]
    args = [w_out.astype(BF16), g_post.reshape(1, d), w_gate.astype(BF16), b_gate.reshape(1, d),
            w_proj.astype(BF16)]
    return args, [_const_spec(a.shape) for a in args]


def _fox_post(gated, x, p, layer, *tail):
    bsz, seq, d = x.shape
    t = POST_ROWS
    row_spec = pl.BlockSpec((1, t, d), lambda b, i: (b, i, 0))
    p_spec = pl.BlockSpec((1, 1, t, PLE_DIM), lambda b, i: (layer, b, i, 0))
    w_args, w_specs = _tail_weights(*tail)
    return pl.pallas_call(
        _fox_post_kernel,
        grid=(bsz, seq // t),
        in_specs=[row_spec, row_spec, p_spec] + w_specs,
        out_specs=row_spec,
        out_shape=jax.ShapeDtypeStruct((bsz, seq, d), F32),
        compiler_params=_params("parallel", "parallel"),
        name="fox_post",
    )(gated, x, p, *w_args)


def _s5_post(yg, z, x, p, layer, w_glu, b_glu, *tail):
    bsz, seq, d = x.shape
    chunks = seq // S5_CHUNK
    rows = bsz * S5_CHUNK
    chunk_spec = pl.BlockSpec((bsz, S5_CHUNK, d), lambda c: (0, c, 0))
    p_spec = pl.BlockSpec((1, bsz, S5_CHUNK, PLE_DIM), lambda c: (layer, 0, c, 0))
    perm = jnp.asarray(_token_major_perm(bsz).T, BF16)
    w_args, w_specs = _tail_weights(*tail)
    return pl.pallas_call(
        _s5_post_kernel,
        grid=(chunks,),
        in_specs=[pl.BlockSpec((S5_GROUPS, 1, bsz, S5_ROW), lambda c: (0, c, 0, 0)),
                  pl.BlockSpec((1, rows, d), lambda c: (c, 0, 0)), chunk_spec, p_spec,
                  _const_spec(perm.shape), _const_spec((d, d)), _const_spec((1, d))] + w_specs,
        out_specs=chunk_spec,
        out_shape=jax.ShapeDtypeStruct((bsz, seq, d), F32),
        compiler_params=_params("parallel"),
        name="s5_post",
    )(yg, z, x, p, perm, w_glu.astype(BF16), b_glu.reshape(1, d), *w_args)


def kernel(x, p, ln_pre, ln_post, fox_w_in, fox_b_f, fox_w_out, s5_w_in, s5_lambda_re, s5_lambda_im,
           s5_log_dt, s5_b_re, s5_b_im, s5_c_re, s5_c_im, s5_d, s5_w_glu, s5_b_glu, s5_w_out,
           ple_w_proj, ple_w_gate, ple_b_gate):
    qa, ka, vt, z = _fox_proj(x, ln_pre[0], fox_w_in[0], fox_b_f[0])
    gated = _fox_attn(qa, ka, vt, z)
    x = _fox_post(gated, x, p, 0, fox_w_out[0], ln_post[0], ple_w_gate[0], ple_b_gate[0],
                  ple_w_proj[0])
    xg, z = _s5_front(x, ln_pre[1], s5_w_in[0])
    ops = _s5_operators(s5_lambda_re[0], s5_lambda_im[0], s5_log_dt[0], s5_b_re[0], s5_b_im[0],
                        s5_c_re[0], s5_c_im[0], s5_d[0])
    yg = _s5_ssm(xg, ops)
    return _s5_post(yg, z, x, p, 1, s5_w_glu[0], s5_b_glu[0], s5_w_out[0], ln_post[1],
                    ple_w_gate[1], ple_b_gate[1], ple_w_proj[1])
```

```python
import functools
import math

import jax
import jax.numpy as jnp
import numpy as np
from jax import lax
from jax.experimental import pallas as pl
from jax.experimental.pallas import tpu as pltpu

F32 = jnp.float32
BF16 = jnp.bfloat16

D_MODEL = 1024
PLE_DIM = 256
RMS_EPS = 1e-6
NEG_INF = -1e30

FOX_HEADS = 16
FOX_HEAD_DIM = 64
FOX_WIDTH = FOX_HEADS * FOX_HEAD_DIM
FOX_AUG_DIM = 128
FOX_AUG_WIDTH = FOX_HEADS * FOX_AUG_DIM

S5_GROUP = 16
S5_GROUPS = D_MODEL // S5_GROUP
S5_STATE = 64
S5_CHUNK = 16
S5_ROW = S5_CHUNK * S5_GROUP
S5_SSM_GROUPS_PER_STEP = 2

V7X_SCOPED_VMEM_LIMIT_BYTES = 56 * 1024 * 1024

PROJ_ROWS = 512
POST_ROWS = 1024
ATTN_Q_ROWS = 256
ATTN_KV_ROWS = 256
ATTN_SCORE_SLOTS = 3
LOG2_E = 1.4426950408889634


def _const_spec(shape):
    zeros = (0,) * len(shape)
    return pl.BlockSpec(shape, lambda *_: zeros, pipeline_mode=pl.Buffered(1))


def _params(*semantics):
    return pltpu.CompilerParams(dimension_semantics=semantics,
                                vmem_limit_bytes=V7X_SCOPED_VMEM_LIMIT_BYTES)


def _dot(a, b):
    return jnp.dot(a, b, preferred_element_type=F32)


def _rmsnorm(x, g):
    var = jnp.mean(x * x, axis=-1, keepdims=True)
    return x * lax.rsqrt(var + RMS_EPS) * g


def _sigmoid(x):
    return 1.0 / (1.0 + jnp.exp(-x))


def _silu(x):
    return x * _sigmoid(x)


def _gelu_tanh(x):
    c = math.sqrt(2.0 / math.pi)
    return 0.5 * x * (1.0 + jnp.tanh(c * (x + 0.044715 * (x * x * x))))


def _split3(x):
    hi = x.astype(BF16).astype(F32)
    r1 = x - hi
    mid = r1.astype(BF16).astype(F32)
    lo = (r1 - mid).astype(BF16).astype(F32)
    return hi, mid, lo


def _fox_proj_kernel(x_ref, g_ref, wq_ref, wk_ref, wv_ref, wz_ref, wf_ref, bf_ref, ek_ref,
                     q_out, k_out, v_out, z_out, carry_ref):
    rows = x_ref.shape[1]

    @pl.when(pl.program_id(1) == 0)
    def _():
        carry_ref[...] = jnp.zeros_like(carry_ref)

    h = _rmsnorm(x_ref[0], g_ref[...]).astype(BF16)

    zf = _dot(h, wf_ref[...]) + bf_ref[...]
    q = _dot(h, wq_ref[...]) * LOG2_E
    z_out[0] = _dot(h, wz_ref[...]).astype(BF16)
    log_f = jnp.minimum(zf, 0.0) - jnp.log(1.0 + jnp.exp(-jnp.abs(zf)))
    r = lax.broadcasted_iota(jnp.int32, (rows, rows), 0)
    c = lax.broadcasted_iota(jnp.int32, (rows, rows), 1)
    tri = jnp.where(r >= c, 1.0, 0.0).astype(BF16)
    hi, mid, lo = _split3(log_f)
    cum = (_dot(tri, hi.astype(BF16)) + _dot(tri, mid.astype(BF16))
           + _dot(tri, lo.astype(BF16))) + carry_ref[0:1, :]
    carry_ref[...] = jnp.broadcast_to(cum[rows - 1:rows, :], carry_ref.shape)

    c_hi, c_mid, c_lo = _split3(cum * LOG2_E)
    lane = lax.broadcasted_iota(jnp.int32, cum.shape, 1)
    aug = jnp.where(lane < 16, c_hi,
                    jnp.where(lane < 32, c_mid,
                              jnp.where(lane < 48, c_lo,
                                        jnp.where(lane < 51, 1.0, 0.0))))

    def widen(qk, upper):
        low = lax.broadcasted_iota(jnp.int32, (rows, FOX_AUG_DIM), 1) < FOX_HEAD_DIM
        tiles = []
        for j in range(FOX_HEADS // 2):
            pair = qk[:, j * FOX_AUG_DIM:(j + 1) * FOX_AUG_DIM]
            for a, val in enumerate((pair, pltpu.roll(pair, FOX_HEAD_DIM, 1))):
                n = (2 * j + a) * FOX_AUG_DIM
                up = upper if upper.shape[1] == FOX_AUG_DIM else upper[:, n:n + FOX_AUG_DIM]
                tiles.append(jnp.where(low, val, up))
        return jnp.concatenate(tiles, axis=1).astype(BF16)

    q_out[0] = widen(q, pltpu.roll(aug, FOX_HEAD_DIM, 1))
    k_out[0] = widen(_dot(h, wk_ref[...]), _dot(aug.astype(BF16), ek_ref[...]))
    vt = lax.dot_general(wv_ref[...], h, (((1,), (1,)), ((), ())),
                         preferred_element_type=F32).astype(BF16)
    for n in range(v_out.shape[1]):
        v_out[0, n] = vt[:, n * ATTN_KV_ROWS:(n + 1) * ATTN_KV_ROWS]


def _fox_k_placement():
    ek = np.zeros((128, FOX_AUG_WIDTH), np.float32)
    for h in range(FOX_HEADS):
        base = h * FOX_AUG_DIM + FOX_HEAD_DIM
        for part in range(3):
            ek[48, base + 16 * part + h] = 1.0
            ek[16 * part + h, base + 48 + part] = -1.0
    return jnp.asarray(ek, BF16)


def _fox_proj(x, g, w_in, b_f):
    bsz, seq, d = x.shape
    scale = FOX_HEAD_DIM ** -0.5
    wq = (w_in[:, :FOX_WIDTH] * scale).astype(BF16)
    wk = w_in[:, FOX_WIDTH:2 * FOX_WIDTH].astype(BF16)
    wv = w_in[:, 2 * FOX_WIDTH:3 * FOX_WIDTH].T.astype(BF16)
    wz = w_in[:, 3 * FOX_WIDTH:4 * FOX_WIDTH].astype(BF16)
    wf = w_in[:, 4 * FOX_WIDTH:]
    wf = jnp.pad(jnp.tile(wf, (1, 3)), ((0, 0), (0, 128 - 3 * FOX_HEADS))).astype(BF16)
    bf = jnp.pad(jnp.tile(b_f, 3), (0, 128 - 3 * FOX_HEADS)).reshape(1, 128).astype(F32)
    ek = _fox_k_placement()

    t = PROJ_ROWS
    tk = ATTN_KV_ROWS
    row_spec = lambda w: pl.BlockSpec((1, t, w), lambda b, i: (b, i, 0))
    return pl.pallas_call(
        _fox_proj_kernel,
        grid=(bsz, seq // t),
        in_specs=[row_spec(d), _const_spec((1, d)),
                  _const_spec(wq.shape), _const_spec(wk.shape), _const_spec(wv.shape),
                  _const_spec(wz.shape), _const_spec(wf.shape), _const_spec(bf.shape),
                  _const_spec(ek.shape)],
        out_specs=[row_spec(FOX_AUG_WIDTH), row_spec(FOX_AUG_WIDTH),
                   pl.BlockSpec((1, t // tk, FOX_WIDTH, tk), lambda b, i: (b, i, 0, 0)),
                   row_spec(FOX_WIDTH)],
        out_shape=[jax.ShapeDtypeStruct((bsz, seq, FOX_AUG_WIDTH), BF16),
                   jax.ShapeDtypeStruct((bsz, seq, FOX_AUG_WIDTH), BF16),
                   jax.ShapeDtypeStruct((bsz, seq // tk, FOX_WIDTH, tk), BF16),
                   jax.ShapeDtypeStruct((bsz, seq, FOX_WIDTH), BF16)],
        scratch_shapes=[pltpu.VMEM((8, 128), F32)],
        compiler_params=_params("parallel", "arbitrary"),
        name="fox_proj",
    )(x, g.reshape(1, d), wq, wk, wv, wz, wf, bf, ek)


def _fox_attn_kernel(q_ref, k_ref, vt_ref, z_ref, o_ref, st_ref, *, tq, tk):
    seq = q_ref.shape[1]
    heads = (0, 1)
    slots = st_ref.shape[0]
    steps = [(i, kb) for i in range(seq // tq) for kb in range(i + 1)]

    def issue_scores(t):
        i, kb = steps[t]
        for a in heads:
            lanes = slice(a * FOX_AUG_DIM, (a + 1) * FOX_AUG_DIM)
            k = k_ref[0, kb * tk:(kb + 1) * tk, lanes]
            q = q_ref[0, i * tq:(i + 1) * tq, lanes]
            st_ref[t % slots, a] = lax.dot_general(k, q, (((1,), (1,)), ((), ())),
                                                   preferred_element_type=F32)

    sum_rows = 16
    rid = lax.broadcasted_iota(jnp.int32, (sum_rows, tk), 0)
    ones_rows = jnp.where(rid == 0, 1.0, 0.0).astype(BF16)

    def update(a, kb, st, stats, masked):
        m, acc = stats
        if masked:
            kv = lax.broadcasted_iota(jnp.int32, st.shape, 0)
            qq = lax.broadcasted_iota(jnp.int32, st.shape, 1)
            st = jnp.where(kv <= qq, st, NEG_INF)
        m_new = jnp.maximum(m, jnp.max(st, axis=0, keepdims=True))
        alpha = jnp.exp2(m - m_new)
        p = jnp.exp2(st - m_new)
        vt = vt_ref[0, kb, a * FOX_HEAD_DIM:(a + 1) * FOX_HEAD_DIM, :]
        vt = jnp.concatenate([vt, ones_rows], axis=0)
        return m_new, alpha * acc + _dot(vt, p.astype(BF16))

    def init():
        return [(jnp.full((1, tq), NEG_INF, F32), jnp.zeros((FOX_HEAD_DIM + sum_rows, tq), F32))
                for _ in heads]

    for t in range(slots - 1):
        issue_scores(t)
    stats = init()
    for t, (i, kb) in enumerate(steps):
        if t + slots - 1 < len(steps):
            issue_scores(t + slots - 1)
        stats = [update(a, kb, st_ref[t % slots, a], stats[a], masked=(kb == i)) for a in heads]
        if kb == i:
            outs = [acc[:FOX_HEAD_DIM] / acc[FOX_HEAD_DIM:FOX_HEAD_DIM + 1] for _, acc in stats]
            o = jnp.concatenate(outs, axis=0).T
            rows = slice(i * tq, (i + 1) * tq)
            o_ref[0, rows, :] = (o * _silu(z_ref[0, rows, :].astype(F32))).astype(BF16)
            stats = init()


def _fox_attn(qa, ka, vt, z):
    bsz, seq, _ = z.shape
    tq = ATTN_Q_ROWS
    tk = ATTN_KV_ROWS
    assert tq == tk and vt.shape == (bsz, seq // tk, FOX_WIDTH, tk)
    pair = 2 * FOX_HEAD_DIM
    seq_spec = lambda w: pl.BlockSpec((1, seq, w), lambda b, j: (b, 0, j))
    return pl.pallas_call(
        functools.partial(_fox_attn_kernel, tq=tq, tk=tk),
        grid=(bsz, FOX_HEADS // 2),
        in_specs=[seq_spec(2 * FOX_AUG_DIM), seq_spec(2 * FOX_AUG_DIM),
                  pl.BlockSpec((1, seq // tk, pair, tk), lambda b, j: (b, 0, j, 0)),
                  seq_spec(pair)],
        out_specs=seq_spec(pair),
        out_shape=jax.ShapeDtypeStruct((bsz, seq, FOX_WIDTH), BF16),
        scratch_shapes=[pltpu.VMEM((ATTN_SCORE_SLOTS, 2, tk, tq), F32)],
        compiler_params=_params("parallel", "parallel"),
        name="fox_attn",
    )(qa, ka, vt, z)


def _slot_masks(batch):
    slot = lax.broadcasted_iota(jnp.int32, (batch, 128), 1) >> 4
    return [slot == q for q in range(128 // S5_GROUP)]


def _s5_front_kernel(x_ref, g_ref, perm_ref, wu_ref, wz_ref, xg_out, z_out):
    batch, _, d = x_ref.shape
    slots = 128 // S5_GROUP
    h = _rmsnorm(x_ref[...].reshape(batch * S5_CHUNK, d), g_ref[...]).astype(BF16)
    h = _dot(perm_ref[...], h).astype(BF16)
    u = _dot(h, wu_ref[...])
    z_out[0] = _dot(h, wz_ref[...]).astype(BF16)
    masks = _slot_masks(batch)
    rolled = []
    for s in range(S5_CHUNK):
        shift = (s % slots) * S5_GROUP
        tiles = [u[s * batch:(s + 1) * batch, j * 128:(j + 1) * 128] for j in range(d // 128)]
        rolled.append([pltpu.roll(t, shift, 1) if shift else t for t in tiles])
    for g in range(S5_GROUPS):
        j, r = divmod(g, slots)
        halves = []
        for k in range(S5_CHUNK // slots):
            acc = rolled[slots * k + (-r) % slots][j]
            for q in range(1, slots):
                acc = jnp.where(masks[q], rolled[slots * k + (q - r) % slots][j], acc)
            halves.append(acc)
        xg_out[g, 0] = jnp.concatenate(halves, axis=1).astype(BF16)


def _token_major_perm(bsz):
    n = np.arange(bsz * S5_CHUNK)
    perm = np.zeros((n.size, n.size), np.float32)
    perm[n, (n % bsz) * S5_CHUNK + n // bsz] = 1.0
    return perm


def _s5_front(x, g, w_in):
    bsz, seq, d = x.shape
    chunks = seq // S5_CHUNK
    rows = bsz * S5_CHUNK
    wu = w_in[:, :d].astype(BF16)
    wz = w_in[:, d:].astype(BF16)
    perm = jnp.asarray(_token_major_perm(bsz), BF16)
    return pl.pallas_call(
        _s5_front_kernel,
        grid=(chunks,),
        in_specs=[pl.BlockSpec((bsz, S5_CHUNK, d), lambda c: (0, c, 0)), _const_spec((1, d)),
                  _const_spec(perm.shape), _const_spec(wu.shape), _const_spec(wz.shape)],
        out_specs=[pl.BlockSpec((S5_GROUPS, 1, bsz, S5_ROW), lambda c: (0, c, 0, 0)),
                   pl.BlockSpec((1, rows, d), lambda c: (c, 0, 0))],
        out_shape=[jax.ShapeDtypeStruct((S5_GROUPS, chunks, bsz, S5_ROW), BF16),
                   jax.ShapeDtypeStruct((chunks, rows, d), BF16)],
        compiler_params=_params("parallel"),
        name="s5_front",
    )(x, g.reshape(1, d), perm, wu, wz)


def _s5_group_kernel(x_ref, w_ref, bst_ref, cst_ref, ar_ref, ai_ref, y_ref,
                     sre_ref, sim_ref, pre_ref, pim_ref, *, batch, chunks):
    groups = range(x_ref.shape[0])
    xs = [x_ref[g] for g in groups]
    for g in groups:
        s = _dot(xs[g], bst_ref[g])
        sre_ref[g] = s[:, :S5_STATE]
        sim_ref[g] = s[:, S5_STATE:]
    ar = [ar_ref[g] for g in groups]
    ai = [ai_ref[g] for g in groups]
    zero = jnp.zeros((batch, S5_STATE), F32)
    state = [(zero, zero) for _ in groups]
    for c in range(chunks):
        rows = slice(c * batch, (c + 1) * batch)
        for g in groups:
            xr, xi = state[g]
            pre_ref[g, rows, :] = xr
            pim_ref[g, rows, :] = xi
            state[g] = (ar[g] * xr - ai[g] * xi + sre_ref[g, rows, :],
                        ar[g] * xi + ai[g] * xr + sim_ref[g, rows, :])
    ys = [_dot(xs[g], w_ref[g]) for g in groups]
    for g in groups:
        prev = jnp.concatenate([pre_ref[g], pim_ref[g]], axis=1).astype(BF16)
        y_ref[g] = ys[g] + _dot(prev, cst_ref[g])


def _s5_operator_kernel(lre_ref, lim_ref, ldt_ref, brt_ref, bit_ref, cre_ref, cim_ref, d_ref,
                        w_ref, bst_ref, cst_ref, ar_ref, ai_ref):
    t = S5_CHUNK
    hp = lax.Precision.HIGHEST
    nt = (((1,), (1,)), ((), ()))
    tau = lax.broadcasted_iota(jnp.int32, (24, S5_STATE), 0).astype(F32)
    row = lax.broadcasted_iota(jnp.int32, (S5_GROUP, S5_ROW), 0)
    lane = lax.broadcasted_iota(jnp.int32, (S5_GROUP, S5_ROW), 1)
    for r in range(w_ref.shape[0]):
        lr, li = lre_ref[r], lim_ref[r]
        dt = jnp.exp(ldt_ref[r])
        mag = jnp.exp(lr * dt * tau)
        ang = li * dt * tau
        pw_re = mag * jnp.cos(ang)
        pw_im = mag * jnp.sin(ang)
        pr = [pw_re[n:n + 1] for n in range(t + 1)]
        pi = [pw_im[n:n + 1] for n in range(t + 1)]
        den = lr * lr + li * li
        num_re = pr[1] - 1.0
        num_im = pi[1]
        coef_re = (num_re * lr + num_im * li) / den
        coef_im = (num_im * lr - num_re * li) / den
        brt, bit = brt_ref[r], bit_ref[r]
        bb_re = coef_re * brt - coef_im * bit
        bb_im = coef_re * bit + coef_im * brt
        cr, ci = cre_ref[r], cim_ref[r]
        c_pw_re = [cr * pr[n] - ci * pi[n] for n in range(t + 1)]
        c_pw_im = [-(cr * pi[n] + ci * pr[n]) for n in range(t + 1)]
        kt = (lax.dot_general(bb_re, jnp.concatenate(c_pw_re[:t], axis=0), nt, precision=hp,
                              preferred_element_type=F32)
              + lax.dot_general(bb_im, jnp.concatenate(c_pw_im[:t], axis=0), nt, precision=hp,
                                preferred_element_type=F32))
        kt = kt + jnp.where(lane == row, d_ref[r], 0.0)
        for s in range(t):
            off = (8 * (s // 8) + (s % 8 + r) % 8) * S5_GROUP
            blk = kt if s == 0 else jnp.where(lane >= s * S5_GROUP,
                                              pltpu.roll(kt, s * S5_GROUP, 1), 0.0)
            w_ref[r, off:off + S5_GROUP, :] = blk.astype(BF16)
            n = t - 1 - s
            m_re = bb_re * pr[n] - bb_im * pi[n]
            m_im = bb_re * pi[n] + bb_im * pr[n]
            bst_ref[r, off:off + S5_GROUP, :] = jnp.concatenate([m_re, m_im], axis=1).astype(BF16)
        readout = jnp.concatenate([jnp.concatenate(c_pw_re[1:], axis=0),
                                   jnp.concatenate(c_pw_im[1:], axis=0)], axis=1)
        cst_ref[r] = readout.T.astype(BF16)
        ar_ref[r] = pr[t]
        ai_ref[r] = pi[t]


def _s5_operators(lam_re, lam_im, log_dt, b_re, b_im, c_re, c_im, d_skip):
    g = S5_GROUPS
    n = 128 // S5_GROUP
    spec = lambda a, b: pl.BlockSpec((n, a, b), lambda j: (j, 0, 0))
    d_row = jnp.pad(d_skip.reshape(g, 1, S5_GROUP), ((0, 0), (0, 0), (0, S5_ROW - S5_GROUP)))
    return pl.pallas_call(
        _s5_operator_kernel,
        grid=(g // n,),
        in_specs=[spec(1, S5_STATE), spec(1, S5_STATE), spec(1, 1), spec(S5_GROUP, S5_STATE),
                  spec(S5_GROUP, S5_STATE), spec(S5_GROUP, S5_STATE), spec(S5_GROUP, S5_STATE),
                  spec(1, S5_ROW)],
        out_specs=[spec(S5_ROW, S5_ROW), spec(S5_ROW, 2 * S5_STATE), spec(2 * S5_STATE, S5_ROW),
                   spec(1, S5_STATE), spec(1, S5_STATE)],
        out_shape=[jax.ShapeDtypeStruct((g, S5_ROW, S5_ROW), BF16),
                   jax.ShapeDtypeStruct((g, S5_ROW, 2 * S5_STATE), BF16),
                   jax.ShapeDtypeStruct((g, 2 * S5_STATE, S5_ROW), BF16),
                   jax.ShapeDtypeStruct((g, 1, S5_STATE), F32),
                   jax.ShapeDtypeStruct((g, 1, S5_STATE), F32)],
        compiler_params=_params("parallel"),
        name="s5_operators",
    )(lam_re.reshape(g, 1, S5_STATE), lam_im.reshape(g, 1, S5_STATE), log_dt.reshape(g, 1, 1),
      b_re.transpose(0, 2, 1), b_im.transpose(0, 2, 1), c_re, c_im, d_row)


def _s5_ssm(xg, ops):
    _, chunks, bsz, _ = xg.shape
    rows = chunks * bsz
    n = S5_SSM_GROUPS_PER_STEP
    g_spec = lambda a, b: pl.BlockSpec((n, a, b), lambda g: (g, 0, 0))
    yg = pl.pallas_call(
        functools.partial(_s5_group_kernel, batch=bsz, chunks=chunks),
        grid=(S5_GROUPS // n,),
        in_specs=[g_spec(rows, S5_ROW), g_spec(S5_ROW, S5_ROW), g_spec(S5_ROW, 2 * S5_STATE),
                  g_spec(2 * S5_STATE, S5_ROW), g_spec(1, S5_STATE), g_spec(1, S5_STATE)],
        out_specs=g_spec(rows, S5_ROW),
        out_shape=jax.ShapeDtypeStruct((S5_GROUPS, rows, S5_ROW), F32),
        scratch_shapes=[pltpu.VMEM((n, rows, S5_STATE), F32)] * 4,
        compiler_params=_params("parallel"),
        name="s5_ssm",
    )(xg.reshape(S5_GROUPS, rows, S5_ROW), *ops)
    return yg.reshape(S5_GROUPS, chunks, bsz, S5_ROW)


def _residual_ple(mix, x, p, wo_ref, g_ref, wg_ref, bg_ref, wp_ref):
    x1 = x + _rmsnorm(_dot(mix, wo_ref[...]), g_ref[...])
    gate = _sigmoid(_dot(x1.astype(BF16), wg_ref[...]) + bg_ref[...])
    return x1 + _dot(p.astype(BF16), wp_ref[...]) * gate


def _fox_post_kernel(a_ref, x_ref, p_ref, wo_ref, g_ref, wg_ref, bg_ref, wp_ref, o_ref):
    o_ref[0] = _residual_ple(a_ref[0], x_ref[0], p_ref[0, 0], wo_ref, g_ref, wg_ref, bg_ref, wp_ref)


def _s5_glu_mix(yg_ref, z_ref, wglu_ref, bglu_ref, batch, d):
    slots = 128 // S5_GROUP
    slot = lax.broadcasted_iota(jnp.int32, (batch, 128), 1) >> 4
    tok = [[None] * (d // 128) for _ in range(S5_CHUNK)]
    for j in range(d // 128):
        for k in range(S5_CHUNK // slots):
            tiles = [yg_ref[slots * j + r, 0, :, k * 128:(k + 1) * 128] for r in range(slots)]
            for dist in (4, 2, 1):
                low = (slot & (2 * dist - 1)) < dist
                nxt = list(tiles)
                for r in range(slots):
                    if not r & dist:
                        a, b = tiles[r], tiles[r + dist]
                        nxt[r] = jnp.where(low, a, pltpu.roll(b, dist * S5_GROUP, 1))
                        nxt[r + dist] = jnp.where(low, pltpu.roll(a, 128 - dist * S5_GROUP, 1), b)
                tiles = nxt
            for tq in range(slots):
                tok[slots * k + tq][j] = tiles[tq]
    y = jnp.concatenate([jnp.concatenate(t, axis=1) for t in tok], axis=0)
    gl = _gelu_tanh(y)
    gl = gl * _sigmoid(_dot(gl.astype(BF16), wglu_ref[...]) + bglu_ref[...])
    return (gl * _silu(z_ref[0].astype(F32))).astype(BF16)


def _s5_post_kernel(yg0_ref, z0_ref, ygn_ref, zn_ref, x_ref, p_ref, perm_ref, wglu_ref, bglu_ref,
                    wo_ref, g_ref, wg_ref, bg_ref, wp_ref, o_ref, mix_ref):
    batch, _, d = x_ref.shape
    rows = batch * S5_CHUNK

    @pl.when(pl.program_id(0) == 0)
    def _():
        mix_ref[...] = _s5_glu_mix(yg0_ref, z0_ref, wglu_ref, bglu_ref, batch, d)

    mix = _dot(perm_ref[...], mix_ref[...]).astype(BF16)
    x1 = x_ref[...].reshape(rows, d) + _rmsnorm(_dot(mix, wo_ref[...]), g_ref[...])
    ple = _dot(p_ref[0].reshape(rows, PLE_DIM).astype(BF16), wp_ref[...])
    gate = _dot(x1.astype(BF16), wg_ref[...])
    mix_ref[...] = _s5_glu_mix(ygn_ref, zn_ref, wglu_ref, bglu_ref, batch, d)
    o_ref[...] = (x1 + ple * _sigmoid(gate + bg_ref[...])).reshape(batch, S5_CHUNK, d)


def _tail_weights(w_out, g_post, w_gate, b_gate, w_proj):
    d = w_out.shape[0]
    args = [w_out.astype(BF16), g_post.reshape(1, d), w_gate.astype(BF16), b_gate.reshape(1, d),
            w_proj.astype(BF16)]
    return args, [_const_spec(a.shape) for a in args]


def _fox_post(gated, x, p, layer, *tail):
    bsz, seq, d = x.shape
    t = POST_ROWS
    row_spec = pl.BlockSpec((1, t, d), lambda b, i: (b, i, 0))
    p_spec = pl.BlockSpec((1, 1, t, PLE_DIM), lambda b, i: (layer, b, i, 0))
    w_args, w_specs = _tail_weights(*tail)
    return pl.pallas_call(
        _fox_post_kernel,
        grid=(bsz, seq // t),
        in_specs=[row_spec, row_spec, p_spec] + w_specs,
        out_specs=row_spec,
        out_shape=jax.ShapeDtypeStruct((bsz, seq, d), F32),
        compiler_params=_params("parallel", "parallel"),
        name="fox_post",
    )(gated, x, p, *w_args)


def _s5_post(yg, z, x, p, layer, w_glu, b_glu, *tail):
    bsz, seq, d = x.shape
    chunks = seq // S5_CHUNK
    rows = bsz * S5_CHUNK
    chunk_spec = pl.BlockSpec((bsz, S5_CHUNK, d), lambda c: (0, c, 0))
    p_spec = pl.BlockSpec((1, bsz, S5_CHUNK, PLE_DIM), lambda c: (layer, 0, c, 0))
    perm = jnp.asarray(_token_major_perm(bsz).T, BF16)
    w_args, w_specs = _tail_weights(*tail)
    yg_block = (S5_GROUPS, 1, bsz, S5_ROW)
    ahead = lambda c: jnp.minimum(c + 1, chunks - 1)
    return pl.pallas_call(
        _s5_post_kernel,
        grid=(chunks,),
        in_specs=[_const_spec(yg_block), _const_spec((1, rows, d)),
                  pl.BlockSpec(yg_block, lambda c: (0, ahead(c), 0, 0)),
                  pl.BlockSpec((1, rows, d), lambda c: (ahead(c), 0, 0)), chunk_spec, p_spec,
                  _const_spec(perm.shape), _const_spec((d, d)), _const_spec((1, d))] + w_specs,
        out_specs=chunk_spec,
        out_shape=jax.ShapeDtypeStruct((bsz, seq, d), F32),
        scratch_shapes=[pltpu.VMEM((rows, d), BF16)],
        compiler_params=_params("arbitrary"),
        name="s5_post",
    )(yg, z, yg, z, x, p, perm, w_glu.astype(BF16), b_glu.reshape(1, d), *w_args)


def kernel(x, p, ln_pre, ln_post, fox_w_in, fox_b_f, fox_w_out, s5_w_in, s5_lambda_re, s5_lambda_im,
           s5_log_dt, s5_b_re, s5_b_im, s5_c_re, s5_c_im, s5_d, s5_w_glu, s5_b_glu, s5_w_out,
           ple_w_proj, ple_w_gate, ple_b_gate):
    qa, ka, vt, z = _fox_proj(x, ln_pre[0], fox_w_in[0], fox_b_f[0])
    gated = _fox_attn(qa, ka, vt, z)
    x = _fox_post(gated, x, p, 0, fox_w_out[0], ln_post[0], ple_w_gate[0], ple_b_gate[0],
                  ple_w_proj[0])
    xg, z = _s5_front(x, ln_pre[1], s5_w_in[0])
    ops = _s5_operators(s5_lambda_re[0], s5_lambda_im[0], s5_log_dt[0], s5_b_re[0], s5_b_im[0],
                        s5_c_re[0], s5_c_im[0], s5_d[0])
    yg = _s5_ssm(xg, ops)
    return _s5_post(yg, z, x, p, 1, s5_w_glu[0], s5_b_glu[0], s5_w_out[0], ln_post[1],
                    ple_w_gate[1], ple_b_gate[1], ple_w_proj[1])
```

```python
import functools
import math

import jax
import jax.numpy as jnp
import numpy as np
from jax import lax
from jax.experimental import pallas as pl
from jax.experimental.pallas import tpu as pltpu

F32 = jnp.float32
BF16 = jnp.bfloat16

D_MODEL = 1024
PLE_DIM = 256
RMS_EPS = 1e-6
NEG_INF = -1e30

FOX_HEADS = 16
FOX_HEAD_DIM = 64
FOX_WIDTH = FOX_HEADS * FOX_HEAD_DIM
FOX_AUG_DIM = 128
FOX_AUG_WIDTH = FOX_HEADS * FOX_AUG_DIM

S5_GROUP = 16
S5_GROUPS = D_MODEL // S5_GROUP
S5_STATE = 64
S5_CHUNK = 16
S5_ROW = S5_CHUNK * S5_GROUP
S5_SSM_GROUPS_PER_STEP = 2

V7X_SCOPED_VMEM_LIMIT_BYTES = 56 * 1024 * 1024

PROJ_ROWS = 512
POST_ROWS = 1024
ATTN_Q_ROWS = 256
ATTN_KV_ROWS = 256
ATTN_SCORE_SLOTS = 3
LOG2_E = 1.4426950408889634


def _const_spec(shape):
    zeros = (0,) * len(shape)
    return pl.BlockSpec(shape, lambda *_: zeros, pipeline_mode=pl.Buffered(1))


def _params(*semantics):
    return pltpu.CompilerParams(dimension_semantics=semantics,
                                vmem_limit_bytes=V7X_SCOPED_VMEM_LIMIT_BYTES)


def _dot(a, b):
    return jnp.dot(a, b, preferred_element_type=F32)


def _rmsnorm(x, g):
    var = jnp.mean(x * x, axis=-1, keepdims=True)
    return x * lax.rsqrt(var + RMS_EPS) * g


def _sigmoid(x):
    return 1.0 / (1.0 + jnp.exp(-x))


def _silu(x):
    return x * _sigmoid(x)


def _gelu_tanh(x):
    c = math.sqrt(2.0 / math.pi)
    return 0.5 * x * (1.0 + jnp.tanh(c * (x + 0.044715 * (x * x * x))))


def _split3(x):
    hi = x.astype(BF16).astype(F32)
    r1 = x - hi
    mid = r1.astype(BF16).astype(F32)
    lo = (r1 - mid).astype(BF16).astype(F32)
    return hi, mid, lo


def _fox_proj_kernel(x_ref, g_ref, wq_ref, wk_ref, wv_ref, wz_ref, wf_ref, bf_ref, ek_ref,
                     q_out, k_out, v_out, z_out, carry_ref):
    rows = x_ref.shape[1]

    @pl.when(pl.program_id(1) == 0)
    def _():
        carry_ref[...] = jnp.zeros_like(carry_ref)

    h = _rmsnorm(x_ref[0], g_ref[...]).astype(BF16)

    zf = _dot(h, wf_ref[...]) + bf_ref[...]
    q = _dot(h, wq_ref[...]) * LOG2_E
    z_out[0] = _dot(h, wz_ref[...]).astype(BF16)
    log_f = jnp.minimum(zf, 0.0) - jnp.log(1.0 + jnp.exp(-jnp.abs(zf)))
    r = lax.broadcasted_iota(jnp.int32, (rows, rows), 0)
    c = lax.broadcasted_iota(jnp.int32, (rows, rows), 1)
    tri = jnp.where(r >= c, 1.0, 0.0).astype(BF16)
    hi, mid, lo = _split3(log_f)
    cum = (_dot(tri, hi.astype(BF16)) + _dot(tri, mid.astype(BF16))
           + _dot(tri, lo.astype(BF16))) + carry_ref[0:1, :]
    carry_ref[...] = jnp.broadcast_to(cum[rows - 1:rows, :], carry_ref.shape)

    c_hi, c_mid, c_lo = _split3(cum * LOG2_E)
    lane = lax.broadcasted_iota(jnp.int32, cum.shape, 1)
    aug = jnp.where(lane < 16, c_hi,
                    jnp.where(lane < 32, c_mid,
                              jnp.where(lane < 48, c_lo,
                                        jnp.where(lane < 51, 1.0, 0.0))))

    def widen(qk, upper):
        low = lax.broadcasted_iota(jnp.int32, (rows, FOX_AUG_DIM), 1) < FOX_HEAD_DIM
        tiles = []
        for j in range(FOX_HEADS // 2):
            pair = qk[:, j * FOX_AUG_DIM:(j + 1) * FOX_AUG_DIM]
            for a, val in enumerate((pair, pltpu.roll(pair, FOX_HEAD_DIM, 1))):
                n = (2 * j + a) * FOX_AUG_DIM
                up = upper if upper.shape[1] == FOX_AUG_DIM else upper[:, n:n + FOX_AUG_DIM]
                tiles.append(jnp.where(low, val, up))
        return jnp.concatenate(tiles, axis=1).astype(BF16)

    q_out[0] = widen(q, pltpu.roll(aug, FOX_HEAD_DIM, 1))
    k_out[0] = widen(_dot(h, wk_ref[...]), _dot(aug.astype(BF16), ek_ref[...]))
    vt = lax.dot_general(wv_ref[...], h, (((1,), (1,)), ((), ())),
                         preferred_element_type=F32).astype(BF16)
    for n in range(v_out.shape[1]):
        v_out[0, n] = vt[:, n * ATTN_KV_ROWS:(n + 1) * ATTN_KV_ROWS]


def _fox_k_placement():
    ek = np.zeros((128, FOX_AUG_WIDTH), np.float32)
    for h in range(FOX_HEADS):
        base = h * FOX_AUG_DIM + FOX_HEAD_DIM
        for part in range(3):
            ek[48, base + 16 * part + h] = 1.0
            ek[16 * part + h, base + 48 + part] = -1.0
    return jnp.asarray(ek, BF16)


def _fox_proj(x, g, w_in, b_f):
    bsz, seq, d = x.shape
    scale = FOX_HEAD_DIM ** -0.5
    wq = (w_in[:, :FOX_WIDTH] * scale).astype(BF16)
    wk = w_in[:, FOX_WIDTH:2 * FOX_WIDTH].astype(BF16)
    wv = w_in[:, 2 * FOX_WIDTH:3 * FOX_WIDTH].T.astype(BF16)
    wz = w_in[:, 3 * FOX_WIDTH:4 * FOX_WIDTH].astype(BF16)
    wf = w_in[:, 4 * FOX_WIDTH:]
    wf = jnp.pad(jnp.tile(wf, (1, 3)), ((0, 0), (0, 128 - 3 * FOX_HEADS))).astype(BF16)
    bf = jnp.pad(jnp.tile(b_f, 3), (0, 128 - 3 * FOX_HEADS)).reshape(1, 128).astype(F32)
    ek = _fox_k_placement()

    t = PROJ_ROWS
    tk = ATTN_KV_ROWS
    row_spec = lambda w: pl.BlockSpec((1, t, w), lambda b, i: (b, i, 0))
    return pl.pallas_call(
        _fox_proj_kernel,
        grid=(bsz, seq // t),
        in_specs=[row_spec(d), _const_spec((1, d)),
                  _const_spec(wq.shape), _const_spec(wk.shape), _const_spec(wv.shape),
                  _const_spec(wz.shape), _const_spec(wf.shape), _const_spec(bf.shape),
                  _const_spec(ek.shape)],
        out_specs=[row_spec(FOX_AUG_WIDTH), row_spec(FOX_AUG_WIDTH),
                   pl.BlockSpec((1, t // tk, FOX_WIDTH, tk), lambda b, i: (b, i, 0, 0)),
                   row_spec(FOX_WIDTH)],
        out_shape=[jax.ShapeDtypeStruct((bsz, seq, FOX_AUG_WIDTH), BF16),
                   jax.ShapeDtypeStruct((bsz, seq, FOX_AUG_WIDTH), BF16),
                   jax.ShapeDtypeStruct((bsz, seq // tk, FOX_WIDTH, tk), BF16),
                   jax.ShapeDtypeStruct((bsz, seq, FOX_WIDTH), BF16)],
        scratch_shapes=[pltpu.VMEM((8, 128), F32)],
        compiler_params=_params("parallel", "arbitrary"),
        name="fox_proj",
    )(x, g.reshape(1, d), wq, wk, wv, wz, wf, bf, ek)


def _fox_attn_kernel(q_ref, k_ref, vt_ref, z_ref, o_ref, st_ref, *, tq, tk):
    seq = q_ref.shape[1]
    heads = (0, 1)
    slots = st_ref.shape[0]
    steps = [(i, kb) for i in range(seq // tq) for kb in range(i + 1)]

    def issue_scores(t):
        i, kb = steps[t]
        for a in heads:
            lanes = slice(a * FOX_AUG_DIM, (a + 1) * FOX_AUG_DIM)
            k = k_ref[0, kb * tk:(kb + 1) * tk, lanes]
            q = q_ref[0, i * tq:(i + 1) * tq, lanes]
            st_ref[t % slots, a] = lax.dot_general(k, q, (((1,), (1,)), ((), ())),
                                                   preferred_element_type=F32)

    sum_rows = 16
    rid = lax.broadcasted_iota(jnp.int32, (sum_rows, tk), 0)
    ones_rows = jnp.where(rid == 0, 1.0, 0.0).astype(BF16)

    def update(a, kb, st, stats, masked):
        m, acc = stats
        if masked:
            kv = lax.broadcasted_iota(jnp.int32, st.shape, 0)
            qq = lax.broadcasted_iota(jnp.int32, st.shape, 1)
            st = jnp.where(kv <= qq, st, NEG_INF)
        m_new = jnp.maximum(m, jnp.max(st, axis=0, keepdims=True))
        alpha = jnp.exp2(m - m_new)
        p = jnp.exp2(st - m_new)
        vt = vt_ref[0, kb, a * FOX_HEAD_DIM:(a + 1) * FOX_HEAD_DIM, :]
        vt = jnp.concatenate([vt, ones_rows], axis=0)
        return m_new, alpha * acc + _dot(vt, p.astype(BF16))

    def init():
        return [(jnp.full((1, tq), NEG_INF, F32), jnp.zeros((FOX_HEAD_DIM + sum_rows, tq), F32))
                for _ in heads]

    for t in range(slots - 1):
        issue_scores(t)
    stats = init()
    for t, (i, kb) in enumerate(steps):
        if t + slots - 1 < len(steps):
            issue_scores(t + slots - 1)
        stats = [update(a, kb, st_ref[t % slots, a], stats[a], masked=(kb == i)) for a in heads]
        if kb == i:
            outs = [acc[:FOX_HEAD_DIM] / acc[FOX_HEAD_DIM:FOX_HEAD_DIM + 1] for _, acc in stats]
            o = jnp.concatenate(outs, axis=0).T
            rows = slice(i * tq, (i + 1) * tq)
            o_ref[0, rows, :] = (o * _silu(z_ref[0, rows, :].astype(F32))).astype(BF16)
            stats = init()


def _fox_attn(qa, ka, vt, z):
    bsz, seq, _ = z.shape
    tq = ATTN_Q_ROWS
    tk = ATTN_KV_ROWS
    assert tq == tk and vt.shape == (bsz, seq // tk, FOX_WIDTH, tk)
    pair = 2 * FOX_HEAD_DIM
    seq_spec = lambda w: pl.BlockSpec((1, seq, w), lambda b, j: (b, 0, j))
    return pl.pallas_call(
        functools.partial(_fox_attn_kernel, tq=tq, tk=tk),
        grid=(bsz, FOX_HEADS // 2),
        in_specs=[seq_spec(2 * FOX_AUG_DIM), seq_spec(2 * FOX_AUG_DIM),
                  pl.BlockSpec((1, seq // tk, pair, tk), lambda b, j: (b, 0, j, 0)),
                  seq_spec(pair)],
        out_specs=seq_spec(pair),
        out_shape=jax.ShapeDtypeStruct((bsz, seq, FOX_WIDTH), BF16),
        scratch_shapes=[pltpu.VMEM((ATTN_SCORE_SLOTS, 2, tk, tq), F32)],
        compiler_params=_params("parallel", "parallel"),
        name="fox_attn",
    )(qa, ka, vt, z)


def _slot_masks(batch):
    slot = lax.broadcasted_iota(jnp.int32, (batch, 128), 1) >> 4
    return [slot == q for q in range(128 // S5_GROUP)]


def _s5_normed_rows(x_ref, g_ref, perm_ref):
    batch, _, d = x_ref.shape
    h = _rmsnorm(x_ref[...].reshape(batch * S5_CHUNK, d), g_ref[...]).astype(BF16)
    return _dot(perm_ref[...], h).astype(BF16)


def _s5_front_kernel(x0_ref, xn_ref, g_ref, perm_ref, wu_ref, wz_ref, xg_out, z_out, h_ref):
    batch, _, d = xn_ref.shape
    slots = 128 // S5_GROUP

    @pl.when(pl.program_id(0) == 0)
    def _():
        h_ref[...] = _s5_normed_rows(x0_ref, g_ref, perm_ref)

    h = h_ref[...]
    u = _dot(h, wu_ref[...])
    z_out[0] = _dot(h, wz_ref[...]).astype(BF16)
    h_ref[...] = _s5_normed_rows(xn_ref, g_ref, perm_ref)
    masks = _slot_masks(batch)
    rolled = []
    for s in range(S5_CHUNK):
        shift = (s % slots) * S5_GROUP
        tiles = [u[s * batch:(s + 1) * batch, j * 128:(j + 1) * 128] for j in range(d // 128)]
        rolled.append([pltpu.roll(t, shift, 1) if shift else t for t in tiles])
    for g in range(S5_GROUPS):
        j, r = divmod(g, slots)
        halves = []
        for k in range(S5_CHUNK // slots):
            acc = rolled[slots * k + (-r) % slots][j]
            for q in range(1, slots):
                acc = jnp.where(masks[q], rolled[slots * k + (q - r) % slots][j], acc)
            halves.append(acc)
        xg_out[g, 0] = jnp.concatenate(halves, axis=1).astype(BF16)


def _token_major_perm(bsz):
    n = np.arange(bsz * S5_CHUNK)
    perm = np.zeros((n.size, n.size), np.float32)
    perm[n, (n % bsz) * S5_CHUNK + n // bsz] = 1.0
    return perm


def _s5_front(x, g, w_in):
    bsz, seq, d = x.shape
    chunks = seq // S5_CHUNK
    rows = bsz * S5_CHUNK
    wu = w_in[:, :d].astype(BF16)
    wz = w_in[:, d:].astype(BF16)
    perm = jnp.asarray(_token_major_perm(bsz), BF16)
    ahead = lambda c: jnp.minimum(c + 1, chunks - 1)
    return pl.pallas_call(
        _s5_front_kernel,
        grid=(chunks,),
        in_specs=[_const_spec((bsz, S5_CHUNK, d)),
                  pl.BlockSpec((bsz, S5_CHUNK, d), lambda c: (0, ahead(c), 0)),
                  _const_spec((1, d)), _const_spec(perm.shape), _const_spec(wu.shape),
                  _const_spec(wz.shape)],
        out_specs=[pl.BlockSpec((S5_GROUPS, 1, bsz, S5_ROW), lambda c: (0, c, 0, 0)),
                   pl.BlockSpec((1, rows, d), lambda c: (c, 0, 0))],
        out_shape=[jax.ShapeDtypeStruct((S5_GROUPS, chunks, bsz, S5_ROW), BF16),
                   jax.ShapeDtypeStruct((chunks, rows, d), BF16)],
        scratch_shapes=[pltpu.VMEM((rows, d), BF16)],
        compiler_params=_params("arbitrary"),
        name="s5_front",
    )(x, x, g.reshape(1, d), perm, wu, wz)


def _s5_group_kernel(x_ref, w_ref, bst_ref, cst_ref, ar_ref, ai_ref, y_ref,
                     sre_ref, sim_ref, pre_ref, pim_ref, *, batch, chunks):
    groups = range(x_ref.shape[0])
    xs = [x_ref[g] for g in groups]
    for g in groups:
        s = _dot(xs[g], bst_ref[g])
        sre_ref[g] = s[:, :S5_STATE]
        sim_ref[g] = s[:, S5_STATE:]
    ar = [ar_ref[g] for g in groups]
    ai = [ai_ref[g] for g in groups]
    zero = jnp.zeros((batch, S5_STATE), F32)
    state = [(zero, zero) for _ in groups]
    for c in range(chunks):
        rows = slice(c * batch, (c + 1) * batch)
        for g in groups:
            xr, xi = state[g]
            pre_ref[g, rows, :] = xr
            pim_ref[g, rows, :] = xi
            state[g] = (ar[g] * xr - ai[g] * xi + sre_ref[g, rows, :],
                        ar[g] * xi + ai[g] * xr + sim_ref[g, rows, :])
    ys = [_dot(xs[g], w_ref[g]) for g in groups]
    for g in groups:
        prev = jnp.concatenate([pre_ref[g], pim_ref[g]], axis=1).astype(BF16)
        y_ref[g] = ys[g] + _dot(prev, cst_ref[g])


def _s5_operator_kernel(lre_ref, lim_ref, ldt_ref, brt_ref, bit_ref, cre_ref, cim_ref, d_ref,
                        w_ref, bst_ref, cst_ref, ar_ref, ai_ref):
    t = S5_CHUNK
    hp = lax.Precision.HIGHEST
    nt = (((1,), (1,)), ((), ()))
    tau = lax.broadcasted_iota(jnp.int32, (24, S5_STATE), 0).astype(F32)
    row = lax.broadcasted_iota(jnp.int32, (S5_GROUP, S5_ROW), 0)
    lane = lax.broadcasted_iota(jnp.int32, (S5_GROUP, S5_ROW), 1)
    for r in range(w_ref.shape[0]):
        lr, li = lre_ref[r], lim_ref[r]
        dt = jnp.exp(ldt_ref[r])
        mag = jnp.exp(lr * dt * tau)
        ang = li * dt * tau
        pw_re = mag * jnp.cos(ang)
        pw_im = mag * jnp.sin(ang)
        pr = [pw_re[n:n + 1] for n in range(t + 1)]
        pi = [pw_im[n:n + 1] for n in range(t + 1)]
        den = lr * lr + li * li
        num_re = pr[1] - 1.0
        num_im = pi[1]
        coef_re = (num_re * lr + num_im * li) / den
        coef_im = (num_im * lr - num_re * li) / den
        brt, bit = brt_ref[r], bit_ref[r]
        bb_re = coef_re * brt - coef_im * bit
        bb_im = coef_re * bit + coef_im * brt
        cr, ci = cre_ref[r], cim_ref[r]
        c_pw_re = [cr * pr[n] - ci * pi[n] for n in range(t + 1)]
        c_pw_im = [-(cr * pi[n] + ci * pr[n]) for n in range(t + 1)]
        kt = (lax.dot_general(bb_re, jnp.concatenate(c_pw_re[:t], axis=0), nt, precision=hp,
                              preferred_element_type=F32)
              + lax.dot_general(bb_im, jnp.concatenate(c_pw_im[:t], axis=0), nt, precision=hp,
                                preferred_element_type=F32))
        kt = kt + jnp.where(lane == row, d_ref[r], 0.0)
        for s in range(t):
            off = (8 * (s // 8) + (s % 8 + r) % 8) * S5_GROUP
            blk = kt if s == 0 else jnp.where(lane >= s * S5_GROUP,
                                              pltpu.roll(kt, s * S5_GROUP, 1), 0.0)
            w_ref[r, off:off + S5_GROUP, :] = blk.astype(BF16)
            n = t - 1 - s
            m_re = bb_re * pr[n] - bb_im * pi[n]
            m_im = bb_re * pi[n] + bb_im * pr[n]
            bst_ref[r, off:off + S5_GROUP, :] = jnp.concatenate([m_re, m_im], axis=1).astype(BF16)
        readout = jnp.concatenate([jnp.concatenate(c_pw_re[1:], axis=0),
                                   jnp.concatenate(c_pw_im[1:], axis=0)], axis=1)
        cst_ref[r] = readout.T.astype(BF16)
        ar_ref[r] = pr[t]
        ai_ref[r] = pi[t]


def _s5_operators(lam_re, lam_im, log_dt, b_re, b_im, c_re, c_im, d_skip):
    g = S5_GROUPS
    n = 128 // S5_GROUP
    spec = lambda a, b: pl.BlockSpec((n, a, b), lambda j: (j, 0, 0))
    d_row = jnp.pad(d_skip.reshape(g, 1, S5_GROUP), ((0, 0), (0, 0), (0, S5_ROW - S5_GROUP)))
    return pl.pallas_call(
        _s5_operator_kernel,
        grid=(g // n,),
        in_specs=[spec(1, S5_STATE), spec(1, S5_STATE), spec(1, 1), spec(S5_GROUP, S5_STATE),
                  spec(S5_GROUP, S5_STATE), spec(S5_GROUP, S5_STATE), spec(S5_GROUP, S5_STATE),
                  spec(1, S5_ROW)],
        out_specs=[spec(S5_ROW, S5_ROW), spec(S5_ROW, 2 * S5_STATE), spec(2 * S5_STATE, S5_ROW),
                   spec(1, S5_STATE), spec(1, S5_STATE)],
        out_shape=[jax.ShapeDtypeStruct((g, S5_ROW, S5_ROW), BF16),
                   jax.ShapeDtypeStruct((g, S5_ROW, 2 * S5_STATE), BF16),
                   jax.ShapeDtypeStruct((g, 2 * S5_STATE, S5_ROW), BF16),
                   jax.ShapeDtypeStruct((g, 1, S5_STATE), F32),
                   jax.ShapeDtypeStruct((g, 1, S5_STATE), F32)],
        compiler_params=_params("parallel"),
        name="s5_operators",
    )(lam_re.reshape(g, 1, S5_STATE), lam_im.reshape(g, 1, S5_STATE), log_dt.reshape(g, 1, 1),
      b_re.transpose(0, 2, 1), b_im.transpose(0, 2, 1), c_re, c_im, d_row)


def _s5_ssm(xg, ops):
    _, chunks, bsz, _ = xg.shape
    rows = chunks * bsz
    n = S5_SSM_GROUPS_PER_STEP
    g_spec = lambda a, b: pl.BlockSpec((n, a, b), lambda g: (g, 0, 0))
    yg = pl.pallas_call(
        functools.partial(_s5_group_kernel, batch=bsz, chunks=chunks),
        grid=(S5_GROUPS // n,),
        in_specs=[g_spec(rows, S5_ROW), g_spec(S5_ROW, S5_ROW), g_spec(S5_ROW, 2 * S5_STATE),
                  g_spec(2 * S5_STATE, S5_ROW), g_spec(1, S5_STATE), g_spec(1, S5_STATE)],
        out_specs=g_spec(rows, S5_ROW),
        out_shape=jax.ShapeDtypeStruct((S5_GROUPS, rows, S5_ROW), F32),
        scratch_shapes=[pltpu.VMEM((n, rows, S5_STATE), F32)] * 4,
        compiler_params=_params("parallel"),
        name="s5_ssm",
    )(xg.reshape(S5_GROUPS, rows, S5_ROW), *ops)
    return yg.reshape(S5_GROUPS, chunks, bsz, S5_ROW)


def _residual_ple(mix, x, p, wo_ref, g_ref, wg_ref, bg_ref, wp_ref):
    x1 = x + _rmsnorm(_dot(mix, wo_ref[...]), g_ref[...])
    gate = _sigmoid(_dot(x1.astype(BF16), wg_ref[...]) + bg_ref[...])
    return x1 + _dot(p.astype(BF16), wp_ref[...]) * gate


def _fox_post_kernel(a_ref, x_ref, p_ref, wo_ref, g_ref, wg_ref, bg_ref, wp_ref, o_ref):
    o_ref[0] = _residual_ple(a_ref[0], x_ref[0], p_ref[0, 0], wo_ref, g_ref, wg_ref, bg_ref, wp_ref)


def _s5_glu_mix(yg_ref, z_ref, wglu_ref, bglu_ref, batch, d):
    slots = 128 // S5_GROUP
    slot = lax.broadcasted_iota(jnp.int32, (batch, 128), 1) >> 4
    tok = [[None] * (d // 128) for _ in range(S5_CHUNK)]
    for j in range(d // 128):
        for k in range(S5_CHUNK // slots):
            tiles = [yg_ref[slots * j + r, 0, :, k * 128:(k + 1) * 128] for r in range(slots)]
            for dist in (4, 2, 1):
                low = (slot & (2 * dist - 1)) < dist
                nxt = list(tiles)
                for r in range(slots):
                    if not r & dist:
                        a, b = tiles[r], tiles[r + dist]
                        nxt[r] = jnp.where(low, a, pltpu.roll(b, dist * S5_GROUP, 1))
                        nxt[r + dist] = jnp.where(low, pltpu.roll(a, 128 - dist * S5_GROUP, 1), b)
                tiles = nxt
            for tq in range(slots):
                tok[slots * k + tq][j] = tiles[tq]
    y = jnp.concatenate([jnp.concatenate(t, axis=1) for t in tok], axis=0)
    gl = _gelu_tanh(y)
    gl = gl * _sigmoid(_dot(gl.astype(BF16), wglu_ref[...]) + bglu_ref[...])
    return (gl * _silu(z_ref[0].astype(F32))).astype(BF16)


def _s5_post_kernel(yg0_ref, z0_ref, ygn_ref, zn_ref, x_ref, p_ref, perm_ref, wglu_ref, bglu_ref,
                    wo_ref, g_ref, wg_ref, bg_ref, wp_ref, o_ref, mix_ref):
    batch, _, d = x_ref.shape
    rows = batch * S5_CHUNK

    @pl.when(pl.program_id(0) == 0)
    def _():
        mix_ref[...] = _s5_glu_mix(yg0_ref, z0_ref, wglu_ref, bglu_ref, batch, d)

    mix = _dot(perm_ref[...], mix_ref[...]).astype(BF16)
    x1 = x_ref[...].reshape(rows, d) + _rmsnorm(_dot(mix, wo_ref[...]), g_ref[...])
    ple = _dot(p_ref[0].reshape(rows, PLE_DIM).astype(BF16), wp_ref[...])
    gate = _dot(x1.astype(BF16), wg_ref[...])
    mix_ref[...] = _s5_glu_mix(ygn_ref, zn_ref, wglu_ref, bglu_ref, batch, d)
    o_ref[...] = (x1 + ple * _sigmoid(gate + bg_ref[...])).reshape(batch, S5_CHUNK, d)


def _tail_weights(w_out, g_post, w_gate, b_gate, w_proj):
    d = w_out.shape[0]
    args = [w_out.astype(BF16), g_post.reshape(1, d), w_gate.astype(BF16), b_gate.reshape(1, d),
            w_proj.astype(BF16)]
    return args, [_const_spec(a.shape) for a in args]


def _fox_post(gated, x, p, layer, *tail):
    bsz, seq, d = x.shape
    t = POST_ROWS
    row_spec = pl.BlockSpec((1, t, d), lambda b, i: (b, i, 0))
    p_spec = pl.BlockSpec((1, 1, t, PLE_DIM), lambda b, i: (layer, b, i, 0))
    w_args, w_specs = _tail_weights(*tail)
    return pl.pallas_call(
        _fox_post_kernel,
        grid=(bsz, seq // t),
        in_specs=[row_spec, row_spec, p_spec] + w_specs,
        out_specs=row_spec,
        out_shape=jax.ShapeDtypeStruct((bsz, seq, d), F32),
        compiler_params=_params("parallel", "parallel"),
        name="fox_post",
    )(gated, x, p, *w_args)


def _s5_post(yg, z, x, p, layer, w_glu, b_glu, *tail):
    bsz, seq, d = x.shape
    chunks = seq // S5_CHUNK
    rows = bsz * S5_CHUNK
    chunk_spec = pl.BlockSpec((bsz, S5_CHUNK, d), lambda c: (0, c, 0))
    p_spec = pl.BlockSpec((1, bsz, S5_CHUNK, PLE_DIM), lambda c: (layer, 0, c, 0))
    perm = jnp.asarray(_token_major_perm(bsz).T, BF16)
    w_args, w_specs = _tail_weights(*tail)
    yg_block = (S5_GROUPS, 1, bsz, S5_ROW)
    ahead = lambda c: jnp.minimum(c + 1, chunks - 1)
    return pl.pallas_call(
        _s5_post_kernel,
        grid=(chunks,),
        in_specs=[_const_spec(yg_block), _const_spec((1, rows, d)),
                  pl.BlockSpec(yg_block, lambda c: (0, ahead(c), 0, 0)),
                  pl.BlockSpec((1, rows, d), lambda c: (ahead(c), 0, 0)), chunk_spec, p_spec,
                  _const_spec(perm.shape), _const_spec((d, d)), _const_spec((1, d))] + w_specs,
        out_specs=chunk_spec,
        out_shape=jax.ShapeDtypeStruct((bsz, seq, d), F32),
        scratch_shapes=[pltpu.VMEM((rows, d), BF16)],
        compiler_params=_params("arbitrary"),
        name="s5_post",
    )(yg, z, yg, z, x, p, perm, w_glu.astype(BF16), b_glu.reshape(1, d), *w_args)


def kernel(x, p, ln_pre, ln_post, fox_w_in, fox_b_f, fox_w_out, s5_w_in, s5_lambda_re, s5_lambda_im,
           s5_log_dt, s5_b_re, s5_b_im, s5_c_re, s5_c_im, s5_d, s5_w_glu, s5_b_glu, s5_w_out,
           ple_w_proj, ple_w_gate, ple_b_gate):
    qa, ka, vt, z = _fox_proj(x, ln_pre[0], fox_w_in[0], fox_b_f[0])
    gated = _fox_attn(qa, ka, vt, z)
    x = _fox_post(gated, x, p, 0, fox_w_out[0], ln_post[0], ple_w_gate[0], ple_b_gate[0],
                  ple_w_proj[0])
    xg, z = _s5_front(x, ln_pre[1], s5_w_in[0])
    ops = _s5_operators(s5_lambda_re[0], s5_lambda_im[0], s5_log_dt[0], s5_b_re[0], s5_b_im[0],
                        s5_c_re[0], s5_c_im[0], s5_d[0])
    yg = _s5_ssm(xg, ops)
    return _s5_post(yg, z, x, p, 1, s5_w_glu[0], s5_b_glu[0], s5_w_out[0], ln_post[1],
                    ple_w_gate[1], ple_b_gate[1], ple_w_proj[1])
```

```python
import functools
import math

import jax
import jax.numpy as jnp
import numpy as np
from jax import lax
from jax.experimental import pallas as pl
from jax.experimental.pallas import tpu as pltpu

F32 = jnp.float32
BF16 = jnp.bfloat16

D_MODEL = 1024
PLE_DIM = 256
RMS_EPS = 1e-6
NEG_INF = -1e30

FOX_HEADS = 16
FOX_HEAD_DIM = 64
FOX_WIDTH = FOX_HEADS * FOX_HEAD_DIM
FOX_AUG_DIM = 128
FOX_AUG_WIDTH = FOX_HEADS * FOX_AUG_DIM

S5_GROUP = 16
S5_GROUPS = D_MODEL // S5_GROUP
S5_STATE = 64
S5_CHUNK = 16
S5_ROW = S5_CHUNK * S5_GROUP
S5_SSM_GROUPS_PER_STEP = 2

V7X_SCOPED_VMEM_LIMIT_BYTES = 56 * 1024 * 1024

PROJ_ROWS = 512
POST_ROWS = 1024
ATTN_Q_ROWS = 256
ATTN_KV_ROWS = 256
ATTN_SCORE_SLOTS = 3
ATTN_HEADS_PER_STEP = 4
LOG2_E = 1.4426950408889634


def _const_spec(shape):
    zeros = (0,) * len(shape)
    return pl.BlockSpec(shape, lambda *_: zeros, pipeline_mode=pl.Buffered(1))


def _params(*semantics):
    return pltpu.CompilerParams(dimension_semantics=semantics,
                                vmem_limit_bytes=V7X_SCOPED_VMEM_LIMIT_BYTES)


def _dot(a, b):
    return jnp.dot(a, b, preferred_element_type=F32)


def _rmsnorm(x, g):
    var = jnp.mean(x * x, axis=-1, keepdims=True)
    return x * lax.rsqrt(var + RMS_EPS) * g


def _sigmoid(x):
    return 1.0 / (1.0 + jnp.exp(-x))


def _silu(x):
    return x * _sigmoid(x)


def _gelu_tanh(x):
    c = math.sqrt(2.0 / math.pi)
    return 0.5 * x * (1.0 + jnp.tanh(c * (x + 0.044715 * (x * x * x))))


def _split3(x):
    hi = x.astype(BF16).astype(F32)
    r1 = x - hi
    mid = r1.astype(BF16).astype(F32)
    lo = (r1 - mid).astype(BF16).astype(F32)
    return hi, mid, lo


def _fox_proj_kernel(x_ref, g_ref, wq_ref, wk_ref, wv_ref, wz_ref, wf_ref, bf_ref, ek_ref,
                     q_out, k_out, v_out, z_out, carry_ref):
    rows = x_ref.shape[1]

    @pl.when(pl.program_id(1) == 0)
    def _():
        carry_ref[...] = jnp.zeros_like(carry_ref)

    h = _rmsnorm(x_ref[0], g_ref[...]).astype(BF16)

    zf = _dot(h, wf_ref[...]) + bf_ref[...]
    q = _dot(h, wq_ref[...]) * LOG2_E
    z_out[0] = _dot(h, wz_ref[...]).astype(BF16)
    log_f = jnp.minimum(zf, 0.0) - jnp.log(1.0 + jnp.exp(-jnp.abs(zf)))
    r = lax.broadcasted_iota(jnp.int32, (rows, rows), 0)
    c = lax.broadcasted_iota(jnp.int32, (rows, rows), 1)
    tri = jnp.where(r >= c, 1.0, 0.0).astype(BF16)
    hi, mid, lo = _split3(log_f)
    cum = (_dot(tri, hi.astype(BF16)) + _dot(tri, mid.astype(BF16))
           + _dot(tri, lo.astype(BF16))) + carry_ref[0:1, :]
    carry_ref[...] = jnp.broadcast_to(cum[rows - 1:rows, :], carry_ref.shape)

    c_hi, c_mid, c_lo = _split3(cum * LOG2_E)
    lane = lax.broadcasted_iota(jnp.int32, cum.shape, 1)
    aug = jnp.where(lane < 16, c_hi,
                    jnp.where(lane < 32, c_mid,
                              jnp.where(lane < 48, c_lo,
                                        jnp.where(lane < 51, 1.0, 0.0))))

    def widen(qk, upper):
        low = lax.broadcasted_iota(jnp.int32, (rows, FOX_AUG_DIM), 1) < FOX_HEAD_DIM
        tiles = []
        for j in range(FOX_HEADS // 2):
            pair = qk[:, j * FOX_AUG_DIM:(j + 1) * FOX_AUG_DIM]
            for a, val in enumerate((pair, pltpu.roll(pair, FOX_HEAD_DIM, 1))):
                n = (2 * j + a) * FOX_AUG_DIM
                up = upper if upper.shape[1] == FOX_AUG_DIM else upper[:, n:n + FOX_AUG_DIM]
                tiles.append(jnp.where(low, val, up))
        return jnp.concatenate(tiles, axis=1).astype(BF16)

    q_out[0] = widen(q, pltpu.roll(aug, FOX_HEAD_DIM, 1))
    k_out[0] = widen(_dot(h, wk_ref[...]), _dot(aug.astype(BF16), ek_ref[...]))
    vt = lax.dot_general(wv_ref[...], h, (((1,), (1,)), ((), ())),
                         preferred_element_type=F32).astype(BF16)
    for n in range(v_out.shape[1]):
        v_out[0, n] = vt[:, n * ATTN_KV_ROWS:(n + 1) * ATTN_KV_ROWS]


def _fox_k_placement():
    ek = np.zeros((128, FOX_AUG_WIDTH), np.float32)
    for h in range(FOX_HEADS):
        base = h * FOX_AUG_DIM + FOX_HEAD_DIM
        for part in range(3):
            ek[48, base + 16 * part + h] = 1.0
            ek[16 * part + h, base + 48 + part] = -1.0
    return jnp.asarray(ek, BF16)


def _fox_proj(x, g, w_in, b_f):
    bsz, seq, d = x.shape
    scale = FOX_HEAD_DIM ** -0.5
    wq = (w_in[:, :FOX_WIDTH] * scale).astype(BF16)
    wk = w_in[:, FOX_WIDTH:2 * FOX_WIDTH].astype(BF16)
    wv = w_in[:, 2 * FOX_WIDTH:3 * FOX_WIDTH].T.astype(BF16)
    wz = w_in[:, 3 * FOX_WIDTH:4 * FOX_WIDTH].astype(BF16)
    wf = w_in[:, 4 * FOX_WIDTH:]
    wf = jnp.pad(jnp.tile(wf, (1, 3)), ((0, 0), (0, 128 - 3 * FOX_HEADS))).astype(BF16)
    bf = jnp.pad(jnp.tile(b_f, 3), (0, 128 - 3 * FOX_HEADS)).reshape(1, 128).astype(F32)
    ek = _fox_k_placement()

    t = PROJ_ROWS
    tk = ATTN_KV_ROWS
    row_spec = lambda w: pl.BlockSpec((1, t, w), lambda b, i: (b, i, 0))
    return pl.pallas_call(
        _fox_proj_kernel,
        grid=(bsz, seq // t),
        in_specs=[row_spec(d), _const_spec((1, d)),
                  _const_spec(wq.shape), _const_spec(wk.shape), _const_spec(wv.shape),
                  _const_spec(wz.shape), _const_spec(wf.shape), _const_spec(bf.shape),
                  _const_spec(ek.shape)],
        out_specs=[row_spec(FOX_AUG_WIDTH), row_spec(FOX_AUG_WIDTH),
                   pl.BlockSpec((1, t // tk, FOX_WIDTH, tk), lambda b, i: (b, i, 0, 0)),
                   row_spec(FOX_WIDTH)],
        out_shape=[jax.ShapeDtypeStruct((bsz, seq, FOX_AUG_WIDTH), BF16),
                   jax.ShapeDtypeStruct((bsz, seq, FOX_AUG_WIDTH), BF16),
                   jax.ShapeDtypeStruct((bsz, seq // tk, FOX_WIDTH, tk), BF16),
                   jax.ShapeDtypeStruct((bsz, seq, FOX_WIDTH), BF16)],
        scratch_shapes=[pltpu.VMEM((8, 128), F32)],
        compiler_params=_params("parallel", "arbitrary"),
        name="fox_proj",
    )(x, g.reshape(1, d), wq, wk, wv, wz, wf, bf, ek)


def _fox_attn_kernel(q_ref, k_ref, vt_ref, z_ref, o_ref, st_ref, *, tq, tk):
    seq = q_ref.shape[1]
    heads = (0, 1)
    slots = st_ref.shape[0]
    pairs = q_ref.shape[2] // (2 * FOX_AUG_DIM)
    steps = [(pr, i, kb) for pr in range(pairs) for i in range(seq // tq) for kb in range(i + 1)]

    def issue_scores(t):
        pr, i, kb = steps[t]
        for a in heads:
            lanes = slice((2 * pr + a) * FOX_AUG_DIM, (2 * pr + a + 1) * FOX_AUG_DIM)
            k = k_ref[0, kb * tk:(kb + 1) * tk, lanes]
            q = q_ref[0, i * tq:(i + 1) * tq, lanes]
            st_ref[t % slots, a] = lax.dot_general(k, q, (((1,), (1,)), ((), ())),
                                                   preferred_element_type=F32)

    sum_rows = 16
    rid = lax.broadcasted_iota(jnp.int32, (sum_rows, tk), 0)
    ones_rows = jnp.where(rid == 0, 1.0, 0.0).astype(BF16)

    def update(head, kb, st, stats, masked):
        m, acc = stats
        if masked:
            kv = lax.broadcasted_iota(jnp.int32, st.shape, 0)
            qq = lax.broadcasted_iota(jnp.int32, st.shape, 1)
            st = jnp.where(kv <= qq, st, NEG_INF)
        m_new = jnp.maximum(m, jnp.max(st, axis=0, keepdims=True))
        alpha = jnp.exp2(m - m_new)
        p = jnp.exp2(st - m_new)
        vt = vt_ref[0, kb, head * FOX_HEAD_DIM:(head + 1) * FOX_HEAD_DIM, :]
        vt = jnp.concatenate([vt, ones_rows], axis=0)
        return m_new, alpha * acc + _dot(vt, p.astype(BF16))

    def init():
        return [(jnp.full((1, tq), NEG_INF, F32), jnp.zeros((FOX_HEAD_DIM + sum_rows, tq), F32))
                for _ in heads]

    for t in range(slots - 1):
        issue_scores(t)
    stats = init()
    for t, (pr, i, kb) in enumerate(steps):
        if t + slots - 1 < len(steps):
            issue_scores(t + slots - 1)
        stats = [update(2 * pr + a, kb, st_ref[t % slots, a], stats[a], masked=(kb == i))
                 for a in heads]
        if kb == i:
            outs = [acc[:FOX_HEAD_DIM] / acc[FOX_HEAD_DIM:FOX_HEAD_DIM + 1] for _, acc in stats]
            o = jnp.concatenate(outs, axis=0).T
            rows = slice(i * tq, (i + 1) * tq)
            lanes = slice(pr * 2 * FOX_HEAD_DIM, (pr + 1) * 2 * FOX_HEAD_DIM)
            o_ref[0, rows, lanes] = (o * _silu(z_ref[0, rows, lanes].astype(F32))).astype(BF16)
            stats = init()


def _fox_attn(qa, ka, vt, z):
    bsz, seq, _ = z.shape
    tq = ATTN_Q_ROWS
    tk = ATTN_KV_ROWS
    assert tq == tk and vt.shape == (bsz, seq // tk, FOX_WIDTH, tk)
    n = ATTN_HEADS_PER_STEP
    width = n * FOX_HEAD_DIM
    seq_spec = lambda w: pl.BlockSpec((1, seq, w), lambda b, j: (b, 0, j))
    return pl.pallas_call(
        functools.partial(_fox_attn_kernel, tq=tq, tk=tk),
        grid=(bsz, FOX_HEADS // n),
        in_specs=[seq_spec(n * FOX_AUG_DIM), seq_spec(n * FOX_AUG_DIM),
                  pl.BlockSpec((1, seq // tk, width, tk), lambda b, j: (b, 0, j, 0)),
                  seq_spec(width)],
        out_specs=seq_spec(width),
        out_shape=jax.ShapeDtypeStruct((bsz, seq, FOX_WIDTH), BF16),
        scratch_shapes=[pltpu.VMEM((ATTN_SCORE_SLOTS, 2, tk, tq), F32)],
        compiler_params=_params("parallel", "parallel"),
        name="fox_attn",
    )(qa, ka, vt, z)


def _slot_masks(batch):
    slot = lax.broadcasted_iota(jnp.int32, (batch, 128), 1) >> 4
    return [slot == q for q in range(128 // S5_GROUP)]


def _s5_normed_rows(x_ref, g_ref, perm_ref):
    batch, _, d = x_ref.shape
    h = _rmsnorm(x_ref[...].reshape(batch * S5_CHUNK, d), g_ref[...]).astype(BF16)
    return _dot(perm_ref[...], h).astype(BF16)


def _s5_front_kernel(x0_ref, xn_ref, g_ref, perm_ref, wu_ref, wz_ref, xg_out, z_out, h_ref):
    batch, _, d = xn_ref.shape
    slots = 128 // S5_GROUP

    @pl.when(pl.program_id(0) == 0)
    def _():
        h_ref[...] = _s5_normed_rows(x0_ref, g_ref, perm_ref)

    h = h_ref[...]
    u = _dot(h, wu_ref[...])
    z_out[0] = _dot(h, wz_ref[...]).astype(BF16)
    h_ref[...] = _s5_normed_rows(xn_ref, g_ref, perm_ref)
    masks = _slot_masks(batch)
    rolled = []
    for s in range(S5_CHUNK):
        shift = (s % slots) * S5_GROUP
        tiles = [u[s * batch:(s + 1) * batch, j * 128:(j + 1) * 128] for j in range(d // 128)]
        rolled.append([pltpu.roll(t, shift, 1) if shift else t for t in tiles])
    for g in range(S5_GROUPS):
        j, r = divmod(g, slots)
        halves = []
        for k in range(S5_CHUNK // slots):
            acc = rolled[slots * k + (-r) % slots][j]
            for q in range(1, slots):
                acc = jnp.where(masks[q], rolled[slots * k + (q - r) % slots][j], acc)
            halves.append(acc)
        xg_out[g, 0] = jnp.concatenate(halves, axis=1).astype(BF16)


def _token_major_perm(bsz):
    n = np.arange(bsz * S5_CHUNK)
    perm = np.zeros((n.size, n.size), np.float32)
    perm[n, (n % bsz) * S5_CHUNK + n // bsz] = 1.0
    return perm


def _s5_front(x, g, w_in):
    bsz, seq, d = x.shape
    chunks = seq // S5_CHUNK
    rows = bsz * S5_CHUNK
    wu = w_in[:, :d].astype(BF16)
    wz = w_in[:, d:].astype(BF16)
    perm = jnp.asarray(_token_major_perm(bsz), BF16)
    ahead = lambda c: jnp.minimum(c + 1, chunks - 1)
    return pl.pallas_call(
        _s5_front_kernel,
        grid=(chunks,),
        in_specs=[_const_spec((bsz, S5_CHUNK, d)),
                  pl.BlockSpec((bsz, S5_CHUNK, d), lambda c: (0, ahead(c), 0)),
                  _const_spec((1, d)), _const_spec(perm.shape), _const_spec(wu.shape),
                  _const_spec(wz.shape)],
        out_specs=[pl.BlockSpec((S5_GROUPS, 1, bsz, S5_ROW), lambda c: (0, c, 0, 0)),
                   pl.BlockSpec((1, rows, d), lambda c: (c, 0, 0))],
        out_shape=[jax.ShapeDtypeStruct((S5_GROUPS, chunks, bsz, S5_ROW), BF16),
                   jax.ShapeDtypeStruct((chunks, rows, d), BF16)],
        scratch_shapes=[pltpu.VMEM((rows, d), BF16)],
        compiler_params=_params("arbitrary"),
        name="s5_front",
    )(x, x, g.reshape(1, d), perm, wu, wz)


def _s5_group_kernel(x_ref, w_ref, bst_ref, cst_ref, ar_ref, ai_ref, y_ref,
                     sre_ref, sim_ref, pre_ref, pim_ref, *, batch, chunks):
    groups = range(x_ref.shape[0])
    xs = [x_ref[g] for g in groups]
    for g in groups:
        s = _dot(xs[g], bst_ref[g])
        sre_ref[g] = s[:, :S5_STATE]
        sim_ref[g] = s[:, S5_STATE:]
    ar = [ar_ref[g] for g in groups]
    ai = [ai_ref[g] for g in groups]
    zero = jnp.zeros((batch, S5_STATE), F32)
    state = [(zero, zero) for _ in groups]
    for c in range(chunks):
        rows = slice(c * batch, (c + 1) * batch)
        for g in groups:
            xr, xi = state[g]
            pre_ref[g, rows, :] = xr
            pim_ref[g, rows, :] = xi
            state[g] = (ar[g] * xr - ai[g] * xi + sre_ref[g, rows, :],
                        ar[g] * xi + ai[g] * xr + sim_ref[g, rows, :])
    ys = [_dot(xs[g], w_ref[g]) for g in groups]
    for g in groups:
        prev = jnp.concatenate([pre_ref[g], pim_ref[g]], axis=1).astype(BF16)
        y_ref[g] = ys[g] + _dot(prev, cst_ref[g])


def _s5_operator_kernel(lre_ref, lim_ref, ldt_ref, brt_ref, bit_ref, cre_ref, cim_ref, d_ref,
                        w_ref, bst_ref, cst_ref, ar_ref, ai_ref):
    t = S5_CHUNK
    hp = lax.Precision.HIGHEST
    nt = (((1,), (1,)), ((), ()))
    tau = lax.broadcasted_iota(jnp.int32, (24, S5_STATE), 0).astype(F32)
    row = lax.broadcasted_iota(jnp.int32, (S5_GROUP, S5_ROW), 0)
    lane = lax.broadcasted_iota(jnp.int32, (S5_GROUP, S5_ROW), 1)
    for r in range(w_ref.shape[0]):
        lr, li = lre_ref[r], lim_ref[r]
        dt = jnp.exp(ldt_ref[r])
        mag = jnp.exp(lr * dt * tau)
        ang = li * dt * tau
        pw_re = mag * jnp.cos(ang)
        pw_im = mag * jnp.sin(ang)
        pr = [pw_re[n:n + 1] for n in range(t + 1)]
        pi = [pw_im[n:n + 1] for n in range(t + 1)]
        den = lr * lr + li * li
        num_re = pr[1] - 1.0
        num_im = pi[1]
        coef_re = (num_re * lr + num_im * li) / den
        coef_im = (num_im * lr - num_re * li) / den
        brt, bit = brt_ref[r], bit_ref[r]
        bb_re = coef_re * brt - coef_im * bit
        bb_im = coef_re * bit + coef_im * brt
        cr, ci = cre_ref[r], cim_ref[r]
        c_pw_re = [cr * pr[n] - ci * pi[n] for n in range(t + 1)]
        c_pw_im = [-(cr * pi[n] + ci * pr[n]) for n in range(t + 1)]
        kt = (lax.dot_general(bb_re, jnp.concatenate(c_pw_re[:t], axis=0), nt, precision=hp,
                              preferred_element_type=F32)
              + lax.dot_general(bb_im, jnp.concatenate(c_pw_im[:t], axis=0), nt, precision=hp,
                                preferred_element_type=F32))
        kt = kt + jnp.where(lane == row, d_ref[r], 0.0)
        for s in range(t):
            off = (8 * (s // 8) + (s % 8 + r) % 8) * S5_GROUP
            blk = kt if s == 0 else jnp.where(lane >= s * S5_GROUP,
                                              pltpu.roll(kt, s * S5_GROUP, 1), 0.0)
            w_ref[r, off:off + S5_GROUP, :] = blk.astype(BF16)
            n = t - 1 - s
            m_re = bb_re * pr[n] - bb_im * pi[n]
            m_im = bb_re * pi[n] + bb_im * pr[n]
            bst_ref[r, off:off + S5_GROUP, :] = jnp.concatenate([m_re, m_im], axis=1).astype(BF16)
        readout = jnp.concatenate([jnp.concatenate(c_pw_re[1:], axis=0),
                                   jnp.concatenate(c_pw_im[1:], axis=0)], axis=1)
        cst_ref[r] = readout.T.astype(BF16)
        ar_ref[r] = pr[t]
        ai_ref[r] = pi[t]


def _s5_operators(lam_re, lam_im, log_dt, b_re, b_im, c_re, c_im, d_skip):
    g = S5_GROUPS
    n = 128 // S5_GROUP
    spec = lambda a, b: pl.BlockSpec((n, a, b), lambda j: (j, 0, 0))
    d_row = jnp.pad(d_skip.reshape(g, 1, S5_GROUP), ((0, 0), (0, 0), (0, S5_ROW - S5_GROUP)))
    return pl.pallas_call(
        _s5_operator_kernel,
        grid=(g // n,),
        in_specs=[spec(1, S5_STATE), spec(1, S5_STATE), spec(1, 1), spec(S5_GROUP, S5_STATE),
                  spec(S5_GROUP, S5_STATE), spec(S5_GROUP, S5_STATE), spec(S5_GROUP, S5_STATE),
                  spec(1, S5_ROW)],
        out_specs=[spec(S5_ROW, S5_ROW), spec(S5_ROW, 2 * S5_STATE), spec(2 * S5_STATE, S5_ROW),
                   spec(1, S5_STATE), spec(1, S5_STATE)],
        out_shape=[jax.ShapeDtypeStruct((g, S5_ROW, S5_ROW), BF16),
                   jax.ShapeDtypeStruct((g, S5_ROW, 2 * S5_STATE), BF16),
                   jax.ShapeDtypeStruct((g, 2 * S5_STATE, S5_ROW), BF16),
                   jax.ShapeDtypeStruct((g, 1, S5_STATE), F32),
                   jax.ShapeDtypeStruct((g, 1, S5_STATE), F32)],
        compiler_params=_params("parallel"),
        name="s5_operators",
    )(lam_re.reshape(g, 1, S5_STATE), lam_im.reshape(g, 1, S5_STATE), log_dt.reshape(g, 1, 1),
      b_re.transpose(0, 2, 1), b_im.transpose(0, 2, 1), c_re, c_im, d_row)


def _s5_ssm(xg, ops):
    _, chunks, bsz, _ = xg.shape
    rows = chunks * bsz
    n = S5_SSM_GROUPS_PER_STEP
    g_spec = lambda a, b: pl.BlockSpec((n, a, b), lambda g: (g, 0, 0))
    yg = pl.pallas_call(
        functools.partial(_s5_group_kernel, batch=bsz, chunks=chunks),
        grid=(S5_GROUPS // n,),
        in_specs=[g_spec(rows, S5_ROW), g_spec(S5_ROW, S5_ROW), g_spec(S5_ROW, 2 * S5_STATE),
                  g_spec(2 * S5_STATE, S5_ROW), g_spec(1, S5_STATE), g_spec(1, S5_STATE)],
        out_specs=g_spec(rows, S5_ROW),
        out_shape=jax.ShapeDtypeStruct((S5_GROUPS, rows, S5_ROW), F32),
        scratch_shapes=[pltpu.VMEM((n, rows, S5_STATE), F32)] * 4,
        compiler_params=_params("parallel"),
        name="s5_ssm",
    )(xg.reshape(S5_GROUPS, rows, S5_ROW), *ops)
    return yg.reshape(S5_GROUPS, chunks, bsz, S5_ROW)


def _residual_ple(mix, x, p, wo_ref, g_ref, wg_ref, bg_ref, wp_ref):
    x1 = x + _rmsnorm(_dot(mix, wo_ref[...]), g_ref[...])
    gate = _sigmoid(_dot(x1.astype(BF16), wg_ref[...]) + bg_ref[...])
    return x1 + _dot(p.astype(BF16), wp_ref[...]) * gate


def _fox_post_kernel(a_ref, x_ref, p_ref, wo_ref, g_ref, wg_ref, bg_ref, wp_ref, o_ref):
    o_ref[0] = _residual_ple(a_ref[0], x_ref[0], p_ref[0, 0], wo_ref, g_ref, wg_ref, bg_ref, wp_ref)


def _s5_glu_mix(yg_ref, z_ref, wglu_ref, bglu_ref, batch, d):
    slots = 128 // S5_GROUP
    slot = lax.broadcasted_iota(jnp.int32, (batch, 128), 1) >> 4
    tok = [[None] * (d // 128) for _ in range(S5_CHUNK)]
    for j in range(d // 128):
        for k in range(S5_CHUNK // slots):
            tiles = [yg_ref[slots * j + r, 0, :, k * 128:(k + 1) * 128] for r in range(slots)]
            for dist in (4, 2, 1):
                low = (slot & (2 * dist - 1)) < dist
                nxt = list(tiles)
                for r in range(slots):
                    if not r & dist:
                        a, b = tiles[r], tiles[r + dist]
                        nxt[r] = jnp.where(low, a, pltpu.roll(b, dist * S5_GROUP, 1))
                        nxt[r + dist] = jnp.where(low, pltpu.roll(a, 128 - dist * S5_GROUP, 1), b)
                tiles = nxt
            for tq in range(slots):
                tok[slots * k + tq][j] = tiles[tq]
    y = jnp.concatenate([jnp.concatenate(t, axis=1) for t in tok], axis=0)
    gl = _gelu_tanh(y)
    gl = gl * _sigmoid(_dot(gl.astype(BF16), wglu_ref[...]) + bglu_ref[...])
    return (gl * _silu(z_ref[0].astype(F32))).astype(BF16)


def _s5_post_kernel(yg0_ref, z0_ref, ygn_ref, zn_ref, x_ref, p_ref, perm_ref, wglu_ref, bglu_ref,
                    wo_ref, g_ref, wg_ref, bg_ref, wp_ref, o_ref, mix_ref):
    batch, _, d = x_ref.shape
    rows = batch * S5_CHUNK

    @pl.when(pl.program_id(0) == 0)
    def _():
        mix_ref[...] = _s5_glu_mix(yg0_ref, z0_ref, wglu_ref, bglu_ref, batch, d)

    mix = _dot(perm_ref[...], mix_ref[...]).astype(BF16)
    x1 = x_ref[...].reshape(rows, d) + _rmsnorm(_dot(mix, wo_ref[...]), g_ref[...])
    ple = _dot(p_ref[0].reshape(rows, PLE_DIM).astype(BF16), wp_ref[...])
    gate = _dot(x1.astype(BF16), wg_ref[...])
    mix_ref[...] = _s5_glu_mix(ygn_ref, zn_ref, wglu_ref, bglu_ref, batch, d)
    o_ref[...] = (x1 + ple * _sigmoid(gate + bg_ref[...])).reshape(batch, S5_CHUNK, d)


def _tail_weights(w_out, g_post, w_gate, b_gate, w_proj):
    d = w_out.shape[0]
    args = [w_out.astype(BF16), g_post.reshape(1, d), w_gate.astype(BF16), b_gate.reshape(1, d),
            w_proj.astype(BF16)]
    return args, [_const_spec(a.shape) for a in args]


def _fox_post(gated, x, p, layer, *tail):
    bsz, seq, d = x.shape
    t = POST_ROWS
    row_spec = pl.BlockSpec((1, t, d), lambda b, i: (b, i, 0))
    p_spec = pl.BlockSpec((1, 1, t, PLE_DIM), lambda b, i: (layer, b, i, 0))
    w_args, w_specs = _tail_weights(*tail)
    return pl.pallas_call(
        _fox_post_kernel,
        grid=(bsz, seq // t),
        in_specs=[row_spec, row_spec, p_spec] + w_specs,
        out_specs=row_spec,
        out_shape=jax.ShapeDtypeStruct((bsz, seq, d), F32),
        compiler_params=_params("parallel", "parallel"),
        name="fox_post",
    )(gated, x, p, *w_args)


def _s5_post(yg, z, x, p, layer, w_glu, b_glu, *tail):
    bsz, seq, d = x.shape
    chunks = seq // S5_CHUNK
    rows = bsz * S5_CHUNK
    chunk_spec = pl.BlockSpec((bsz, S5_CHUNK, d), lambda c: (0, c, 0))
    p_spec = pl.BlockSpec((1, bsz, S5_CHUNK, PLE_DIM), lambda c: (layer, 0, c, 0))
    perm = jnp.asarray(_token_major_perm(bsz).T, BF16)
    w_args, w_specs = _tail_weights(*tail)
    yg_block = (S5_GROUPS, 1, bsz, S5_ROW)
    ahead = lambda c: jnp.minimum(c + 1, chunks - 1)
    return pl.pallas_call(
        _s5_post_kernel,
        grid=(chunks,),
        in_specs=[_const_spec(yg_block), _const_spec((1, rows, d)),
                  pl.BlockSpec(yg_block, lambda c: (0, ahead(c), 0, 0)),
                  pl.BlockSpec((1, rows, d), lambda c: (ahead(c), 0, 0)), chunk_spec, p_spec,
                  _const_spec(perm.shape), _const_spec((d, d)), _const_spec((1, d))] + w_specs,
        out_specs=chunk_spec,
        out_shape=jax.ShapeDtypeStruct((bsz, seq, d), F32),
        scratch_shapes=[pltpu.VMEM((rows, d), BF16)],
        compiler_params=_params("arbitrary"),
        name="s5_post",
    )(yg, z, yg, z, x, p, perm, w_glu.astype(BF16), b_glu.reshape(1, d), *w_args)


def kernel(x, p, ln_pre, ln_post, fox_w_in, fox_b_f, fox_w_out, s5_w_in, s5_lambda_re, s5_lambda_im,
           s5_log_dt, s5_b_re, s5_b_im, s5_c_re, s5_c_im, s5_d, s5_w_glu, s5_b_glu, s5_w_out,
           ple_w_proj, ple_w_gate, ple_b_gate):
    qa, ka, vt, z = _fox_proj(x, ln_pre[0], fox_w_in[0], fox_b_f[0])
    gated = _fox_attn(qa, ka, vt, z)
    x = _fox_post(gated, x, p, 0, fox_w_out[0], ln_post[0], ple_w_gate[0], ple_b_gate[0],
                  ple_w_proj[0])
    xg, z = _s5_front(x, ln_pre[1], s5_w_in[0])
    ops = _s5_operators(s5_lambda_re[0], s5_lambda_im[0], s5_log_dt[0], s5_b_re[0], s5_b_im[0],
                        s5_c_re[0], s5_c_im[0], s5_d[0])
    yg = _s5_ssm(xg, ops)
    return _s5_post(yg, z, x, p, 1, s5_w_glu[0], s5_b_glu[0], s5_w_out[0], ln_post[1],
                    ple_w_gate[1], ple_b_gate[1], ple_w_proj[1])
```
